```python
import math
import numpy as np
import jax, jax.numpy as jnp
from jax import lax

D_MODEL = 4096
BATCH = 4
SEQ = 2048
DEPTH = 2

HEAD_DIM = 128
HG_HEADS = 8
HG_DK = 128
HG_DV = 128
LB_FLOOR = 1e-30
GD_HEADS = 8
GD_DK = 128
GD_DV = 128
CONV_W = 4
NSA_HEADS = 16
NSA_KV = 4
NSA_HPG = NSA_HEADS // NSA_KV
CMP_LEN = 32
CMP_STRIDE = 16
CMP_HIDDEN = 256
SLC_LEN = 64
SLC_TOP = 16
WINDOW = 512
WIN_BLOCK = 128
SLC_Q_CHUNK = 32
CHUNK = 64
ROPE_THETA = 10000.0
NORM_EPS = 1e-6
NEG = -1e30
FFN_HIDDEN = -(-8 * D_MODEL // (3 * 256)) * 256

HG_W = HG_HEADS * HG_DK
GD_W = GD_HEADS * GD_DK
NSA_W = NSA_HEADS * HEAD_DIM
NSA_KV_W = NSA_KV * HEAD_DIM
MIX_W = HG_W + GD_W + NSA_W
GD_CONV_C = GD_HEADS * (2 * GD_DK + GD_DV)
IN_SIZES = (HG_W, HG_W, HG_HEADS * HG_DV, HG_HEADS * HG_DV,
            GD_W, GD_W, GD_HEADS * GD_DV, GD_HEADS * GD_DV,
            GD_HEADS, GD_HEADS,
            NSA_W,
            NSA_KV_W, NSA_KV_W, NSA_KV_W, NSA_KV_W, NSA_KV_W, NSA_KV_W,
            NSA_HEADS * 3)
N_IN = sum(IN_SIZES)

kernel_name = "hymba_hgrn2_gdn_nsa_hybrid"


def _rmsnorm(x, w):
    xf = x.astype(jnp.float32)
    y = xf * lax.rsqrt(jnp.mean(xf * xf, axis=-1, keepdims=True) + NORM_EPS)
    return (y * w.astype(jnp.float32)).astype(x.dtype)


def _rope_tables(T):
    inv = ROPE_THETA ** (-jnp.arange(0, HEAD_DIM, 2, dtype=jnp.float32) / HEAD_DIM)
    ang = jnp.arange(T, dtype=jnp.float32)[:, None] * inv[None, :]
    return jnp.cos(ang), jnp.sin(ang)


def _rope(x, cos, sin):
    half = HEAD_DIM // 2
    xf = x.astype(jnp.float32)
    x1, x2 = xf[..., :half], xf[..., half:]
    c, s = cos[:, None, :], sin[:, None, :]
    return jnp.concatenate([x1 * c - x2 * s, x2 * c + x1 * s], axis=-1).astype(x.dtype)


def _chunks(a):
    B, T, H = a.shape[:3]
    a = a.reshape((B, T // CHUNK, CHUNK, H) + a.shape[3:])
    return jnp.transpose(a, (1, 0, 3, 2) + tuple(range(4, a.ndim)))


def _unchunks(a):
    nC, B, H, C, d = a.shape
    return jnp.transpose(a, (1, 0, 3, 2, 4)).reshape(B, nC * C, H, d)


def _hgrn2_step(S, inp):
    q, k, v, lf = inp
    C = q.shape[-2]
    b = jnp.cumsum(lf, axis=-2)
    causal = jnp.tril(jnp.ones((C, C), dtype=bool))
    diff = b[..., :, None, :] - b[..., None, :, :]
    dec = jnp.exp(jnp.where(causal[:, :, None], diff, -jnp.inf))
    att = jnp.einsum('bhtd,bhsd,bhtsd->bhts', q, k, dec)
    o = (jnp.einsum('bhtd,bhde->bhte', q * jnp.exp(b), S)
         + jnp.einsum('bhts,bhse->bhte', att, v))
    b_last = b[..., -1:, :]
    S = (jnp.exp(b_last[..., 0, :])[..., None] * S
         + jnp.einsum('bhsd,bhse->bhde', k * jnp.exp(b_last - b), v))
    return S, o


def _hgrn2(q, f_pre, i, g, lb, norm_w):
    B, T, _ = q.shape
    f32 = jnp.float32
    qf = jax.nn.silu(q.astype(f32)).reshape(B, T, HG_HEADS, HG_DK)
    fp = f_pre.astype(f32)
    log_lb = jnp.log(jnp.maximum(lb, LB_FLOOR))
    logf = jnp.minimum(jax.nn.log_sigmoid(fp) + jax.nn.softplus(log_lb - fp), 0.0)
    kf = -jnp.expm1(logf)
    logf = logf.reshape(B, T, HG_HEADS, HG_DK)
    kf = kf.reshape(B, T, HG_HEADS, HG_DK)
    vf = i.astype(f32).reshape(B, T, HG_HEADS, HG_DV)
    S0 = jnp.zeros((B, HG_HEADS, HG_DK, HG_DV), f32)
    _, o = lax.scan(_hgrn2_step, S0, (_chunks(qf), _chunks(kf), _chunks(vf), _chunks(logf)))
    o = _rmsnorm(_unchunks(o), norm_w)
    o = o * jax.nn.silu(g.astype(f32)).reshape(B, T, HG_HEADS, HG_DV)
    return o.reshape(B, T, HG_HEADS * HG_DV).astype(q.dtype)


def _gdn_step(S, inp):
    qd, qk, u, w, kd, dl = inp
    v_new = u - jnp.einsum('bhck,bhkv->bhcv', w, S)
    o = jnp.einsum('bhck,bhkv->bhcv', qd, S) + jnp.einsum('bhts,bhsv->bhtv', qk, v_new)
    S = dl[..., None, None] * S + jnp.einsum('bhck,bhcv->bhkv', kd, v_new)
    return S, o


def _gdn(q, k, v, z, a, b, conv_w, A_log, dt_bias, norm_w):
    B, T, _ = q.shape
    f32 = jnp.float32
    qkv = jnp.concatenate([q, k, v], axis=-1)
    qkv = lax.conv_general_dilated(qkv, conv_w[:, None, :].astype(qkv.dtype), window_strides=(1,),
                                   padding=[(CONV_W - 1, 0)],
                                   dimension_numbers=('NWC', 'WIO', 'NWC'),
                                   feature_group_count=GD_CONV_C)
    qkv = jax.nn.silu(qkv.astype(f32))
    qf, kf, vf = jnp.split(qkv, [GD_W, 2 * GD_W], axis=-1)
    qf = qf.reshape(B, T, GD_HEADS, GD_DK)
    kf = kf.reshape(B, T, GD_HEADS, GD_DK)
    vf = vf.reshape(B, T, GD_HEADS, GD_DV)
    qf = qf * lax.rsqrt(jnp.sum(qf * qf, -1, keepdims=True) + NORM_EPS) * (GD_DK ** -0.5)
    kf = kf * lax.rsqrt(jnp.sum(kf * kf, -1, keepdims=True) + NORM_EPS)
    beta = jax.nn.sigmoid(b.astype(f32))
    g = -jnp.exp(A_log.astype(f32)) * jax.nn.softplus(a.astype(f32) + dt_bias.astype(f32))
    qc, kc, vc = _chunks(qf), _chunks(kf), _chunks(vf)
    bc = _chunks(beta[..., None])[..., 0]
    gc = jnp.cumsum(_chunks(g[..., None])[..., 0], axis=-1)
    C = CHUNK
    incl = jnp.tril(jnp.ones((C, C), dtype=bool))
    strict = jnp.tril(jnp.ones((C, C), dtype=bool), -1)
    L = jnp.exp(jnp.where(incl, gc[..., :, None] - gc[..., None, :], -jnp.inf))
    kk = jnp.einsum('nbhik,nbhjk->nbhij', kc, kc)
    M = jnp.where(strict, bc[..., :, None] * kk * L, 0.0)
    A = M + jnp.eye(C, dtype=f32)
    rhs = jnp.concatenate([vc * bc[..., None], kc * (bc * jnp.exp(gc))[..., None]], axis=-1)
    sol = lax.linalg.triangular_solve(A, rhs, left_side=True, lower=True, unit_diagonal=True)
    u, w = sol[..., :GD_DV], sol[..., GD_DV:]
    qk = jnp.where(incl, jnp.einsum('nbhik,nbhjk->nbhij', qc, kc) * L, 0.0)
    qd = qc * jnp.exp(gc)[..., None]
    g_last = gc[..., -1]
    kd = kc * jnp.exp(g_last[..., None] - gc)[..., None]
    dl = jnp.exp(g_last)
    S0 = jnp.zeros((B, GD_HEADS, GD_DK, GD_DV), f32)
    _, o = lax.scan(_gdn_step, S0, (qd, qk, u, w, kd, dl))
    o = _rmsnorm(_unchunks(o), norm_w)
    o = o * jax.nn.silu(z.astype(f32)).reshape(B, T, GD_HEADS, GD_DV)
    return o.reshape(B, T, GD_HEADS * GD_DV).astype(q.dtype)


def _nsa(q, kc, vc, ks, vs, kw, vw, gate, pos_k, w1k, w2k, pos_v, w1v, w2v, cos, sin):
    B, T = q.shape[:2]
    dt = q.dtype
    f32 = jnp.float32
    G, hpg, dh = NSA_KV, NSA_HPG, HEAD_DIM
    scale = dh ** -0.5
    tpos = jnp.arange(T)
    q = _rope(q.reshape(B, T, NSA_HEADS, dh), cos, sin)
    q = q.reshape(B, T, G, hpg, dh).transpose(0, 2, 3, 1, 4)

    def heads(a, rot):
        a = a.reshape(B, T, G, dh)
        if rot:
            a = _rope(a, cos, sin)
        return a.transpose(0, 2, 1, 3)

    kc, vc = heads(kc, True), heads(vc, False)
    ks, vs = heads(ks, True), heads(vs, False)
    kw, vw = heads(kw, True), heads(vw, False)

    n_cmp = (T - CMP_LEN) // CMP_STRIDE + 1
    cstart = jnp.arange(n_cmp) * CMP_STRIDE
    bidx = cstart[:, None] + jnp.arange(CMP_LEN)[None, :]

    def compress(a, pos, w1, w2):
        blk = (a[:, :, bidx] + pos).reshape(B, G, n_cmp, CMP_LEN * dh)
        return jax.nn.gelu(blk @ w1) @ w2

    k_cmp = compress(kc, pos_k, w1k, w2k)
    v_cmp = compress(vc, pos_v, w1v, w2v)
    s = jnp.einsum('bghtd,bgnd->bghtn', q, k_cmp).astype(f32) * scale
    cmask = (cstart + CMP_LEN - 1)[None, :] <= tpos[:, None]
    p_cmp = jnp.where(cmask, jax.nn.softmax(jnp.where(cmask, s, NEG), axis=-1), 0.0)
    o_cmp = jnp.einsum('bghtn,bgnd->bghtd', p_cmp.astype(dt), v_cmp)

    n_slc = T // SLC_LEN
    sstart = jnp.arange(n_slc) * SLC_LEN
    overlap = jnp.clip(jnp.minimum(cstart[:, None] + CMP_LEN, sstart[None, :] + SLC_LEN)
                       - jnp.maximum(cstart[:, None], sstart[None, :]), 0).astype(f32) / CMP_LEN
    imp = jnp.einsum('bghtn,nj->bgtj', p_cmp, overlap)
    blk = jnp.arange(n_slc)[None, :]
    cur = (tpos // SLC_LEN)[:, None]
    forced = (blk == 0) | (blk == cur) | (blk == cur - 1)
    imp = jnp.where(forced, jnp.inf, imp)
    imp = jnp.where(sstart[None, :] <= tpos[:, None], imp, -jnp.inf)
    n_top = min(SLC_TOP, n_slc)
    _, sel = lax.top_k(imp, n_top)
    ks_b = ks.reshape(B, G, n_slc, SLC_LEN, dh)
    vs_b = vs.reshape(B, G, n_slc, SLC_LEN, dh)
    nqc = T // SLC_Q_CHUNK
    q_c = jnp.moveaxis(q.reshape(B, G, hpg, nqc, SLC_Q_CHUNK, dh), 3, 0)
    sel_c = jnp.moveaxis(sel.reshape(B, G, nqc, SLC_Q_CHUNK, n_top), 2, 0)
    t_c = tpos.reshape(nqc, SLC_Q_CHUNK)
    bi = jnp.arange(B)[:, None, None, None]
    gi = jnp.arange(G)[None, :, None, None]

    def slc_block(args):
        qb, sb, tb = args
        kg = ks_b[bi, gi, sb]
        vg = vs_b[bi, gi, sb]
        sc = jnp.einsum('bghqd,bgqkld->bghqkl', qb, kg).astype(f32) * scale
        kpos = sb[..., None] * SLC_LEN + jnp.arange(SLC_LEN)
        m = kpos <= tb[None, None, :, None, None]
        sc = jnp.where(m[:, :, None], sc, NEG)
        shp = sc.shape
        p = jax.nn.softmax(sc.reshape(shp[:4] + (-1,)), axis=-1).reshape(shp)
        return jnp.einsum('bghqkl,bgqkld->bghqd', p.astype(dt), vg)

    o_slc = lax.map(slc_block, (q_c, sel_c, t_c))
    o_slc = jnp.moveaxis(o_slc, 0, 3).reshape(B, G, hpg, T, dh)

    nwb = T // WIN_BLOCK
    slab = jnp.arange(nwb)[:, None] * WIN_BLOCK + jnp.arange(WINDOW + WIN_BLOCK)[None, :]
    kw_b = jnp.pad(kw, ((0, 0), (0, 0), (WINDOW, 0), (0, 0)))[:, :, slab]
    vw_b = jnp.pad(vw, ((0, 0), (0, 0), (WINDOW, 0), (0, 0)))[:, :, slab]
    qw = q.reshape(B, G, hpg, nwb, WIN_BLOCK, dh)
    sw = jnp.einsum('bghnqd,bgnsd->bghnqs', qw, kw_b).astype(f32) * scale
    qpos = (jnp.arange(nwb)[:, None] * WIN_BLOCK + jnp.arange(WIN_BLOCK)[None, :])[:, :, None]
    kpos = (slab - WINDOW)[:, None, :]
    wmask = (kpos <= qpos) & (kpos > qpos - WINDOW) & (kpos >= 0)
    pw = jax.nn.softmax(jnp.where(wmask, sw, NEG), axis=-1)
    o_win = jnp.einsum('bghnqs,bgnsd->bghnqd', pw.astype(dt), vw_b).reshape(B, G, hpg, T, dh)

    gts = jax.nn.sigmoid(gate.astype(f32)).reshape(B, T, G, hpg, 3).transpose(0, 2, 3, 1, 4)
    gts = gts.astype(dt)
    o = gts[..., 0:1] * o_cmp + gts[..., 1:2] * o_slc + gts[..., 2:3] * o_win
    return o.transpose(0, 3, 1, 2, 4).reshape(B, T, NSA_W)


def setup_inputs(seed: int = 0) -> dict:
    key = jax.random.key(seed)
    ks = jax.random.split(key, 24)
    f32 = jnp.float32

    def nrm(k, shape, scale):
        return jax.random.normal(k, shape, f32) * scale

    dt = jnp.exp(jax.random.uniform(ks[14], (DEPTH, GD_HEADS), f32, math.log(1e-3), math.log(1e-1)))
    return {
        "x": nrm(ks[0], (BATCH, SEQ, D_MODEL), 1.0),
        "attn_norm": 1.0 + nrm(ks[1], (DEPTH, D_MODEL), 0.02),
        "w_in": nrm(ks[2], (DEPTH, D_MODEL, N_IN), D_MODEL ** -0.5),
        "w_out": nrm(ks[3], (DEPTH, MIX_W, D_MODEL), MIX_W ** -0.5),
        "ffn_norm": 1.0 + nrm(ks[4], (DEPTH, D_MODEL), 0.02),
        "w_gate": nrm(ks[5], (DEPTH, D_MODEL, FFN_HIDDEN), D_MODEL ** -0.5),
        "w_up": nrm(ks[6], (DEPTH, D_MODEL, FFN_HIDDEN), D_MODEL ** -0.5),
        "w_down": nrm(ks[7], (DEPTH, FFN_HIDDEN, D_MODEL), FFN_HIDDEN ** -0.5),
        "final_norm": 1.0 + nrm(ks[8], (D_MODEL,), 0.02),
        "hgrn_lb_logits": nrm(ks[9], (DEPTH, HG_W), 1.0),
        "hgrn_out_norm": 1.0 + nrm(ks[10], (DEPTH, HG_DV), 0.02),
        "gdn_conv": nrm(ks[11], (DEPTH, CONV_W, GD_CONV_C), CONV_W ** -0.5),
        "gdn_A_log": jnp.log(jax.random.uniform(ks[12], (DEPTH, GD_HEADS), f32, 1.0, 16.0)),
        "gdn_dt_bias": dt + jnp.log(-jnp.expm1(-dt)),
        "gdn_out_norm": 1.0 + nrm(ks[13], (DEPTH, GD_DV), 0.02),
        "cmp_pos_k": nrm(ks[15], (DEPTH, CMP_LEN, HEAD_DIM), 0.1),
        "cmp_w1_k": nrm(ks[16], (DEPTH, CMP_LEN * HEAD_DIM, CMP_HIDDEN), (CMP_LEN * HEAD_DIM) ** -0.5),
        "cmp_w2_k": nrm(ks[17], (DEPTH, CMP_HIDDEN, HEAD_DIM), CMP_HIDDEN ** -0.5),
        "cmp_pos_v": nrm(ks[18], (DEPTH, CMP_LEN, HEAD_DIM), 0.1),
        "cmp_w1_v": nrm(ks[19], (DEPTH, CMP_LEN * HEAD_DIM, CMP_HIDDEN), (CMP_LEN * HEAD_DIM) ** -0.5),
        "cmp_w2_v": nrm(ks[20], (DEPTH, CMP_HIDDEN, HEAD_DIM), CMP_HIDDEN ** -0.5),
    }


def reference(x, attn_norm, w_in, w_out, ffn_norm, w_gate, w_up, w_down, final_norm,
              hgrn_lb_logits, hgrn_out_norm, gdn_conv, gdn_A_log, gdn_dt_bias, gdn_out_norm,
              cmp_pos_k, cmp_w1_k, cmp_w2_k, cmp_pos_v, cmp_w1_v, cmp_w2_v):
    T = x.shape[1]
    cos, sin = _rope_tables(T)
    lb_p = jax.nn.softmax(hgrn_lb_logits.astype(jnp.float32), axis=0)
    lb_all = jnp.concatenate([jnp.zeros_like(lb_p[:1]), jnp.cumsum(lb_p, axis=0)[:-1]], axis=0)
    split_at = [int(v) for v in np.cumsum(IN_SIZES)[:-1]]
    for l in range(DEPTH):
        h = _rmsnorm(x, attn_norm[l])
        parts = jnp.split(h @ w_in[l], split_at, axis=-1)
        (hq, hf, hi, hg, gq, gk, gv, gz, ga, gb,
         nq, nkc, nvc, nks, nvs, nkw, nvw, ngate) = parts
        o_a = _hgrn2(hq, hf, hi, hg, lb_all[l], hgrn_out_norm[l])
        o_b = _gdn(gq, gk, gv, gz, ga, gb, gdn_conv[l], gdn_A_log[l], gdn_dt_bias[l], gdn_out_norm[l])
        o_c = _nsa(nq, nkc, nvc, nks, nvs, nkw, nvw, ngate,
                   cmp_pos_k[l], cmp_w1_k[l], cmp_w2_k[l], cmp_pos_v[l], cmp_w1_v[l], cmp_w2_v[l],
                   cos, sin)
        x = x + jnp.concatenate([o_a, o_b, o_c], axis=-1) @ w_out[l]
        h = _rmsnorm(x, ffn_norm[l])
        x = x + (jax.nn.silu(h @ w_gate[l]) * (h @ w_up[l])) @ w_down[l]
    return _rmsnorm(x, final_norm)
```

```python
import functools

import jax
import jax.numpy as jnp
from jax import lax
from jax.experimental import pallas as pl
from jax.experimental.pallas import tpu as pltpu

F32 = jnp.float32
BF16 = jnp.bfloat16
HI = lax.Precision.HIGHEST

D_MODEL = 4096
HEAD_DIM = 128
HG_HEADS = 8
GD_HEADS = 8
CONV_W = 4
NSA_HEADS = 16
NSA_KV = 4
NSA_HPG = NSA_HEADS // NSA_KV
CMP_LEN = 32
CMP_STRIDE = 16
CMP_HIDDEN = 256
SLC_LEN = 64
SLC_TOP = 16
WINDOW = 512
ROPE_THETA = 10000.0
NORM_EPS = 1e-6
LB_FLOOR = 1e-30
NEG = -1e30
FFN_HIDDEN = 11008

HG_W = HG_HEADS * HEAD_DIM
GD_W = GD_HEADS * HEAD_DIM
NSA_W = NSA_HEADS * HEAD_DIM
NSA_KV_W = NSA_KV * HEAD_DIM
SM_W = 128
GATE_COL0 = 2 * GD_HEADS

LANES = 128
MIB = 1 << 20

NT = (((1,), (1,)), ((), ()))
TN = (((0,), (0,)), ((), ()))


def _params(sem, vmem_mib):
    return pltpu.CompilerParams(dimension_semantics=sem, vmem_limit_bytes=vmem_mib * MIB)


def _silu(x):
    return x * jax.nn.sigmoid(x)


def _softplus(x):
    return jnp.maximum(x, 0.0) + jnp.log1p(jnp.exp(-jnp.abs(x)))


def _lane_pick(x, lane_ids, idx):
    return jnp.sum(jnp.where(lane_ids == idx, x, 0.0), axis=-1, keepdims=True)


def _rms_kernel(x_ref, w_ref, o_ref):
    x = x_ref[...]
    ms = jnp.mean(x * x, axis=-1, keepdims=True)
    o_ref[...] = (x * lax.rsqrt(ms + NORM_EPS) * w_ref[...]).astype(o_ref.dtype)


def _rmsnorm(x, w, out_dtype, rows=256):
    n, d = x.shape
    return pl.pallas_call(
        _rms_kernel,
        grid=(n // rows,),
        in_specs=[pl.BlockSpec((rows, d), lambda i: (i, 0)),
                  pl.BlockSpec((1, d), lambda i: (0, 0))],
        out_specs=pl.BlockSpec((rows, d), lambda i: (i, 0)),
        out_shape=jax.ShapeDtypeStruct((n, d), out_dtype),
        compiler_params=_params(("parallel",), 40),
        name="rmsnorm",
    )(x, w.reshape(1, d))


def _mm_kernel(a_ref, b_ref, o_ref):
    o_ref[...] = jnp.dot(a_ref[...], b_ref[...], preferred_element_type=F32).astype(o_ref.dtype)


def _mm_res_kernel(a_ref, b_ref, r_ref, o_ref):
    o_ref[...] = r_ref[...] + jnp.dot(a_ref[...], b_ref[...], preferred_element_type=F32)


def _mm_acc_res_kernel(a_ref, b_ref, r_ref, o_ref, acc_ref, *, nk):
    k = pl.program_id(2)

    @pl.when(k == 0)
    def _():
        acc_ref[...] = r_ref[...]

    acc_ref[...] += jnp.dot(a_ref[...], b_ref[...], preferred_element_type=F32)

    @pl.when(k == nk - 1)
    def _():
        o_ref[...] = acc_ref[...]


def _matmul(a, b, res=None, tm=1024, tn=512, out_dtype=F32):
    m, k = a.shape
    n = b.shape[1]
    tn = min(tn, n)
    in_specs = [pl.BlockSpec((tm, k), lambda i, j: (i, 0)),
                pl.BlockSpec((k, tn), lambda i, j: (0, j))]
    args = [a, b]
    kern = _mm_kernel
    if res is not None:
        in_specs.append(pl.BlockSpec((tm, tn), lambda i, j: (i, j)))
        args.append(res)
        kern = _mm_res_kernel
    return pl.pallas_call(
        kern,
        grid=(m // tm, n // tn),
        in_specs=in_specs,
        out_specs=pl.BlockSpec((tm, tn), lambda i, j: (i, j)),
        out_shape=jax.ShapeDtypeStruct((m, n), out_dtype),
        compiler_params=_params(("parallel", "parallel"), 48),
        name="matmul",
    )(*args)


def _matmul_ktiled_res(a, b, res, tm=1024, tn=512, tk=2816):
    m, k = a.shape
    n = b.shape[1]
    nk = k // tk
    return pl.pallas_call(
        functools.partial(_mm_acc_res_kernel, nk=nk),
        grid=(m // tm, n // tn, nk),
        in_specs=[pl.BlockSpec((tm, tk), lambda i, j, kk: (i, kk)),
                  pl.BlockSpec((tk, tn), lambda i, j, kk: (kk, j)),
                  pl.BlockSpec((tm, tn), lambda i, j, kk: (i, j))],
        out_specs=pl.BlockSpec((tm, tn), lambda i, j, kk: (i, j)),
        out_shape=jax.ShapeDtypeStruct((m, n), F32),
        scratch_shapes=[pltpu.VMEM((tm, tn), F32)],
        compiler_params=_params(("parallel", "parallel", "arbitrary"), 48),
        name="matmul_down",
    )(a, b, res)


def _gate_up_kernel(a_ref, wg_ref, wu_ref, o_ref):
    a = a_ref[...]
    g = jnp.dot(a, wg_ref[...], preferred_element_type=F32)
    u = jnp.dot(a, wu_ref[...], preferred_element_type=F32)
    o_ref[...] = (_silu(g) * u).astype(o_ref.dtype)


def _gate_up(a, wg, wu, tm=1024, tn=512):
    m, k = a.shape
    n = wg.shape[1]
    return pl.pallas_call(
        _gate_up_kernel,
        grid=(m // tm, n // tn),
        in_specs=[pl.BlockSpec((tm, k), lambda i, j: (i, 0)),
                  pl.BlockSpec((k, tn), lambda i, j: (0, j)),
                  pl.BlockSpec((k, tn), lambda i, j: (0, j))],
        out_specs=pl.BlockSpec((tm, tn), lambda i, j: (i, j)),
        out_shape=jax.ShapeDtypeStruct((m, n), BF16),
        compiler_params=_params(("parallel", "parallel"), 56),
        name="ffn_gate_up",
    )(a, wg, wu)


HG_ROWS = 64
HG_SUB = 16


def _hgrn_kernel(q_ref, f_ref, i_ref, g_ref, lb_ref, nw_ref, o_ref, *, seq):
    rows, sub = HG_ROWS, HG_SUB
    row_id = lax.broadcasted_iota(jnp.int32, (rows, LANES), 0)
    sub_id = row_id & (sub - 1)
    r2 = lax.broadcasted_iota(jnp.int32, (rows, rows), 0)
    c2 = lax.broadcasted_iota(jnp.int32, (rows, rows), 1)
    tri = jnp.where(((r2 // sub) == (c2 // sub)) & (c2 <= r2), 1.0, 0.0).astype(F32)
    log_lb = jnp.log(jnp.maximum(lb_ref[...], LB_FLOOR))
    nw = nw_ref[...]

    def step(c, state):
        r0 = pl.multiple_of(c * rows, rows)
        q = _silu(q_ref[pl.ds(r0, rows), :])
        fp = f_ref[pl.ds(r0, rows), :]
        v = i_ref[pl.ds(r0, rows), :]
        logf = jnp.minimum(-_softplus(-fp) + _softplus(log_lb - fp), 0.0)
        kf = 1.0 - jnp.exp(logf)
        b = jnp.dot(tri, logf, precision=HI, preferred_element_type=F32)

        acc = jnp.sum(q * kf, axis=-1, keepdims=True) * v
        for d in range(1, sub):
            ks = pltpu.roll(kf, d, 0)
            bs = pltpu.roll(b, d, 0)
            vs = pltpu.roll(v, d, 0)
            p = jnp.where(sub_id >= d, q * ks * jnp.exp(jnp.minimum(b - bs, 0.0)), 0.0)
            acc = acc + jnp.sum(p, axis=-1, keepdims=True) * vs

        inter = []
        for j in range(rows // sub):
            sl = slice(j * sub, (j + 1) * sub)
            bj = b[sl]
            bl = bj[sub - 1:sub]
            qt = (q[sl] * jnp.exp(bj)).astype(BF16)
            kh = (kf[sl] * jnp.exp(bl - bj)).astype(BF16)
            inter.append(lax.dot_general(qt, state.astype(BF16), NT, preferred_element_type=F32))
            state = jnp.exp(bl) * state + lax.dot_general(
                v[sl].astype(BF16), kh, TN, preferred_element_type=F32)
        o = acc + jnp.concatenate(inter, axis=0)

        ms = jnp.mean(o * o, axis=-1, keepdims=True)
        y = o * lax.rsqrt(ms + NORM_EPS) * nw
        y = y * _silu(g_ref[pl.ds(r0, rows), :])
        o_ref[pl.ds(r0, rows), :] = y.astype(o_ref.dtype)
        return state

    lax.fori_loop(0, seq // rows, step, jnp.zeros((LANES, LANES), F32))


def _hgrn(p_hg, lb, norm_w):
    bsz, seq, _ = p_hg.shape
    nh = HG_HEADS

    def part(k):
        return pl.BlockSpec((None, seq, LANES), lambda b, h, k=k: (b, 0, h + nh * k))

    return pl.pallas_call(
        functools.partial(_hgrn_kernel, seq=seq),
        grid=(bsz, nh),
        in_specs=[part(0), part(1), part(2), part(3),
                  pl.BlockSpec((1, LANES), lambda b, h: (0, h)),
                  pl.BlockSpec((1, LANES), lambda b, h: (0, 0))],
        out_specs=pl.BlockSpec((None, seq, LANES), lambda b, h: (b, 0, h)),
        out_shape=jax.ShapeDtypeStruct((bsz, seq, HG_W), BF16),
        compiler_params=_params(("parallel", "parallel"), 32),
        name="hgrn2",
    )(p_hg, p_hg, p_hg, p_hg, lb.reshape(1, HG_W), norm_w.reshape(1, LANES))


GD_CHUNK = 64


def _gdn_kernel(q_ref, k_ref, v_ref, z_ref, sm_ref, cq_ref, ck_ref, cv_ref, al_ref, dtb_ref, nw_ref,
                o_ref, u_s, w_s, qd_s, kd_s, qk_s, dl_s, *, seq):
    ch = GD_CHUNK
    nchunks = seq // ch
    h = pl.program_id(1)
    lane1 = lax.broadcasted_iota(jnp.int32, (1, LANES), 1)
    lane = lax.broadcasted_iota(jnp.int32, (ch, LANES), 1)
    neg_a = -jnp.exp(_lane_pick(al_ref[...], lane1, h))
    dt_bias = _lane_pick(dtb_ref[...], lane1, h)
    r2 = lax.broadcasted_iota(jnp.int32, (ch, ch), 0)
    c2 = lax.broadcasted_iota(jnp.int32, (ch, ch), 1)
    incl = c2 <= r2
    strict = c2 < r2
    tri = jnp.where(incl, 1.0, 0.0).astype(F32)
    eye = jnp.where(c2 == r2, 1.0, 0.0).astype(F32)
    first_lane = jnp.where(lane == 0, 1.0, 0.0).astype(F32)
    nw = nw_ref[...]

    def conv_silu(ref, w_ref, r0, c):
        cur = ref[pl.ds(r0, ch), :]
        prev = ref[pl.ds(jnp.maximum(r0 - 8, 0), 8), :]
        prev = jnp.where(c > 0, prev, 0.0)
        x = jnp.concatenate([prev, cur], axis=0)
        w = w_ref[...]
        y = cur * w[CONV_W - 1:CONV_W]
        for s in range(1, CONV_W):
            y = y + pltpu.roll(x, s, 0)[8:] * w[CONV_W - 1 - s:CONV_W - s]
        return _silu(y)

    def prepare(c, carry):
        r0 = pl.multiple_of(c * ch, ch)
        qf = conv_silu(q_ref, cq_ref, r0, c)
        kf = conv_silu(k_ref, ck_ref, r0, c)
        vf = conv_silu(v_ref, cv_ref, r0, c)
        qf = qf * lax.rsqrt(jnp.sum(qf * qf, axis=-1, keepdims=True) + NORM_EPS) * (HEAD_DIM ** -0.5)
        kf = kf * lax.rsqrt(jnp.sum(kf * kf, axis=-1, keepdims=True) + NORM_EPS)
        sm = sm_ref[pl.ds(r0, ch), :]
        a = _lane_pick(sm, lane, h)
        beta = jax.nn.sigmoid(_lane_pick(sm, lane, h + GD_HEADS))
        g = neg_a * _softplus(a + dt_bias)
        gc = jnp.dot(tri, jnp.broadcast_to(g, (ch, LANES)), precision=HI,
                     preferred_element_type=F32)
        gc_row = lax.dot_general(first_lane, gc, NT, precision=HI,
                                 preferred_element_type=F32)
        decay = jnp.where(incl, jnp.exp(jnp.minimum(gc[:, :ch] - gc_row, 0.0)), 0.0)
        kb = kf.astype(BF16)
        kk = lax.dot_general(kb, kb, NT, preferred_element_type=F32)
        nmat = jnp.where(strict, -(beta * kk * decay), 0.0)
        inv = eye + nmat
        pw = nmat
        for _ in range(5):
            pw = jnp.dot(pw, pw, precision=HI, preferred_element_type=F32)
            inv = inv + jnp.dot(inv, pw, precision=HI, preferred_element_type=F32)
        egc = jnp.exp(gc)
        u = jnp.dot(inv, vf * beta, precision=HI, preferred_element_type=F32)
        w = jnp.dot(inv, kf * (beta * egc), precision=HI, preferred_element_type=F32)
        qk = jnp.where(incl, lax.dot_general(qf.astype(BF16), kb, NT, preferred_element_type=F32) * decay, 0.0)
        g_last = gc[ch - 1:ch]
        u_s[pl.ds(r0, ch), :] = u
        w_s[pl.ds(r0, ch), :] = w
        qd_s[pl.ds(r0, ch), :] = qf * egc
        kd_s[pl.ds(r0, ch), :] = kf * jnp.exp(g_last - gc)
        qk_s[pl.ds(r0, ch), :] = qk
        dl_s[pl.ds(pl.multiple_of(c * 8, 8), 8), :] = jnp.broadcast_to(jnp.exp(g_last), (8, LANES))
        return carry

    lax.fori_loop(0, nchunks, prepare, 0)

    def recur(c, state):
        r0 = pl.multiple_of(c * ch, ch)
        sb = state.astype(BF16)
        v_new = u_s[pl.ds(r0, ch), :] - jnp.dot(w_s[pl.ds(r0, ch), :].astype(BF16), sb,
                                                preferred_element_type=F32)
        vb = v_new.astype(BF16)
        o = (jnp.dot(qd_s[pl.ds(r0, ch), :].astype(BF16), sb, preferred_element_type=F32)
             + jnp.dot(qk_s[pl.ds(r0, ch), :].astype(BF16), vb, preferred_element_type=F32))
        dl = dl_s[pl.ds(pl.multiple_of(c * 8, 8), 1), :]
        state = dl * state + lax.dot_general(kd_s[pl.ds(r0, ch), :].astype(BF16), vb, TN,
                                             preferred_element_type=F32)
        ms = jnp.mean(o * o, axis=-1, keepdims=True)
        y = o * lax.rsqrt(ms + NORM_EPS) * nw
        y = y * _silu(z_ref[pl.ds(r0, ch), :])
        o_ref[pl.ds(r0, ch), :] = y.astype(o_ref.dtype)
        return state

    lax.fori_loop(0, nchunks, recur, jnp.zeros((LANES, LANES), F32))


def _gdn(p_gd, p_sm, conv_w, a_log, dt_bias, norm_w):
    bsz, seq, _ = p_gd.shape
    nh = GD_HEADS

    def part(k):
        return pl.BlockSpec((None, seq, LANES), lambda b, h, k=k: (b, 0, h + nh * k))

    def cpart(k):
        return pl.BlockSpec((CONV_W, LANES), lambda b, h, k=k: (0, h + nh * k))

    pad = lambda t: jnp.pad(t.astype(F32), (0, LANES - nh)).reshape(1, LANES)
    small = pl.BlockSpec((1, LANES), lambda b, h: (0, 0))
    return pl.pallas_call(
        functools.partial(_gdn_kernel, seq=seq),
        grid=(bsz, nh),
        in_specs=[part(0), part(1), part(2), part(3),
                  pl.BlockSpec((None, seq, SM_W), lambda b, h: (b, 0, 0)),
                  cpart(0), cpart(1), cpart(2), small, small, small],
        out_specs=pl.BlockSpec((None, seq, LANES), lambda b, h: (b, 0, h)),
        out_shape=jax.ShapeDtypeStruct((bsz, seq, GD_W), BF16),
        scratch_shapes=[pltpu.VMEM((seq, LANES), F32)] * 4
        + [pltpu.VMEM((seq, GD_CHUNK), F32), pltpu.VMEM((seq // GD_CHUNK * 8, LANES), F32)],
        compiler_params=_params(("parallel", "parallel"), 40),
        name="gdn",
    )(p_gd, p_gd, p_gd, p_gd, p_sm, conv_w, conv_w, conv_w, pad(a_log), pad(dt_bias),
      norm_w.reshape(1, LANES))


NSA_TQ = 128
NSA_TK = 256
ROPE_ROWS = 256


def _gelu_tanh(x):
    return 0.5 * x * (1.0 + jnp.tanh(0.7978845608028654 * (x + 0.044715 * x * x * x)))


def _nsa_prep_kernel(kc_ref, vc_ref, ks_ref, vs_ref, kw_ref, vw_ref, cos_ref, sin_ref,
                     posk_ref, w1k_ref, w2k_ref, posv_ref, w1v_ref, w2v_ref,
                     kcmp_ref, vcmp_ref, ksr_ref, vsb_ref, kwr_ref, vwb_ref, kcr_s, *, seq):
    ncmp = seq // CMP_STRIDE
    per = CMP_LEN // CMP_STRIDE

    def rope_rows(i, carry):
        r0 = pl.multiple_of(i * ROPE_ROWS, ROPE_ROWS)
        rs = pl.ds(r0, ROPE_ROWS)
        cosf = cos_ref[rs, :]
        sins = sin_ref[rs, :]

        def rope(x):
            return x * cosf + pltpu.roll(x, HEAD_DIM // 2, 1) * sins

        kcr_s[rs, :] = rope(kc_ref[rs, :])
        ksr_ref[rs, :] = rope(ks_ref[rs, :]).astype(BF16)
        kwr_ref[rs, :] = rope(kw_ref[rs, :]).astype(BF16)
        vsb_ref[rs, :] = vs_ref[rs, :].astype(BF16)
        vwb_ref[rs, :] = vw_ref[rs, :].astype(BF16)
        return carry

    lax.fori_loop(0, seq // ROPE_ROWS, rope_rows, 0)

    def compress(src_ref, pos_ref, w1_ref, w2_ref, out_ref):
        parts = [jnp.zeros((ncmp, CMP_HIDDEN), F32) for _ in range(per)]
        for j in range(CMP_STRIDE):
            xj = src_ref[pl.ds(j, ncmp, stride=CMP_STRIDE), :]
            for p in range(per):
                jj = p * CMP_STRIDE + j
                xp = (xj + pos_ref[jj:jj + 1, :]).astype(BF16)
                parts[p] = parts[p] + jnp.dot(xp, w1_ref[jj * HEAD_DIM:(jj + 1) * HEAD_DIM, :],
                                              preferred_element_type=F32)
        pre = parts[0]
        for p in range(1, per):
            pre = pre + pltpu.roll(parts[p], ncmp - p, 0)
        hid = _gelu_tanh(pre)
        out_ref[...] = jnp.dot(hid.astype(BF16), w2_ref[...], preferred_element_type=F32).astype(BF16)

    compress(kcr_s, posk_ref, w1k_ref, w2k_ref, kcmp_ref)
    compress(vc_ref, posv_ref, w1v_ref, w2v_ref, vcmp_ref)


def _nsa_prep(p_nsa, cosf, sins, posk, w1k, w2k, posv, w1v, w2v):
    bsz, seq, _ = p_nsa.shape
    ng = NSA_KV
    ncmp = seq // CMP_STRIDE
    q_blocks = NSA_W // LANES

    def part(k):
        return pl.BlockSpec((None, seq, LANES), lambda b, g, k=k: (b, 0, q_blocks + ng * k + g))

    def whole(shape):
        return pl.BlockSpec(shape, lambda b, g: (0,) * len(shape))

    seq_out = pl.BlockSpec((None, None, seq, LANES), lambda b, g: (b, g, 0, 0))
    cmp_out = pl.BlockSpec((None, None, ncmp, LANES), lambda b, g: (b, g, 0, 0))
    seq_shape = jax.ShapeDtypeStruct((bsz, ng, seq, LANES), BF16)
    cmp_shape = jax.ShapeDtypeStruct((bsz, ng, ncmp, LANES), BF16)
    return pl.pallas_call(
        functools.partial(_nsa_prep_kernel, seq=seq),
        grid=(bsz, ng),
        in_specs=[part(0), part(1), part(2), part(3), part(4), part(5),
                  whole((seq, LANES)), whole((seq, LANES)),
                  whole((CMP_LEN, HEAD_DIM)), whole((CMP_LEN * HEAD_DIM, CMP_HIDDEN)), whole((CMP_HIDDEN, HEAD_DIM)),
                  whole((CMP_LEN, HEAD_DIM)), whole((CMP_LEN * HEAD_DIM, CMP_HIDDEN)), whole((CMP_HIDDEN, HEAD_DIM))],
        out_specs=[cmp_out, cmp_out, seq_out, seq_out, seq_out, seq_out],
        out_shape=[cmp_shape, cmp_shape, seq_shape, seq_shape, seq_shape, seq_shape],
        scratch_shapes=[pltpu.VMEM((seq, LANES), F32)],
        compiler_params=_params(("parallel", "parallel"), 48),
        name="nsa_prep",
    )(p_nsa, p_nsa, p_nsa, p_nsa, p_nsa, p_nsa, cosf, sins, posk, w1k, w2k, posv, w1v, w2v)


def _nsa_attn_kernel(q_ref, cos_ref, sin_ref, kcmp_ref, vcmp_ref, ks_ref, vs_ref, kw_ref, vw_ref,
                     sm_ref, o_ref, *, seq):
    tq, tk, hpg = NSA_TQ, NSA_TK, NSA_HPG
    rows = hpg * tq
    scale = HEAD_DIM ** -0.5
    g = pl.program_id(1)
    t0 = pl.program_id(2) * tq

    cosf = cos_ref[...]
    sins = sin_ref[...]
    heads = []
    for hh in range(hpg):
        x = q_ref[:, hh * HEAD_DIM:(hh + 1) * HEAD_DIM]
        heads.append((x * cosf + pltpu.roll(x, HEAD_DIM // 2, 1) * sins).astype(BF16))
    q = jnp.concatenate(heads, axis=0)
    t_row = t0 + (lax.broadcasted_iota(jnp.int32, (rows, 1), 0) & (tq - 1))

    ncmp = seq // CMP_STRIDE
    s = lax.dot_general(q, kcmp_ref[...], NT, preferred_element_type=F32) * scale
    n_lane = lax.broadcasted_iota(jnp.int32, (1, ncmp), 1)
    cmask = (n_lane * CMP_STRIDE + (CMP_LEN - 1)) <= t_row
    s = jnp.where(cmask, s, NEG)
    e = jnp.exp(s - jnp.max(s, axis=-1, keepdims=True))
    p_cmp = jnp.where(cmask, e / jnp.sum(e, axis=-1, keepdims=True), 0.0)
    o_cmp = jnp.dot(p_cmp.astype(BF16), vcmp_ref[...], preferred_element_type=F32)

    p_grp = p_cmp[0:tq]
    for hh in range(1, hpg):
        p_grp = p_grp + p_cmp[hh * tq:(hh + 1) * tq]
    n_i = lax.broadcasted_iota(jnp.int32, (ncmp, LANES), 0)
    j_i = lax.broadcasted_iota(jnp.int32, (ncmp, LANES), 1)
    ovl = (jnp.minimum(n_i * CMP_STRIDE + CMP_LEN, j_i * SLC_LEN + SLC_LEN)
           - jnp.maximum(n_i * CMP_STRIDE, j_i * SLC_LEN))
    ovl = jnp.maximum(ovl, 0).astype(F32) * (1.0 / CMP_LEN)
    imp = jnp.dot(p_grp, ovl, precision=HI, preferred_element_type=F32)
    t_q = t0 + lax.broadcasted_iota(jnp.int32, (tq, 1), 0)
    blk = lax.broadcasted_iota(jnp.int32, (1, LANES), 1)
    cur = t_q // SLC_LEN
    nslc = seq // SLC_LEN
    imp = jnp.where((blk == 0) | (blk == cur) | (blk == cur - 1), jnp.inf, imp)
    imp = jnp.where((blk * SLC_LEN <= t_q) & (blk < nslc), imp, -jnp.inf)
    rank = jnp.zeros((tq, LANES), F32)
    for i in range(nslc):
        ci = imp[:, i:i + 1]
        rank = rank + jnp.where((ci > imp) | ((ci == imp) & (blk > i)), 1.0, 0.0)
    sel = jnp.where((rank < float(min(SLC_TOP, nslc))) & (blk < nslc), 1.0, 0.0).astype(BF16)

    key_lane = lax.broadcasted_iota(jnp.int32, (1, tk), 1)
    e_row = lax.broadcasted_iota(jnp.int32, (LANES, tk), 0)
    e_col = lax.broadcasted_iota(jnp.int32, (LANES, tk), 1)

    def attend(k_ref, v_ref, kb, carry, mask_fn):
        m_i, l_i, acc = carry
        k0 = pl.multiple_of(kb * tk, tk)
        s = lax.dot_general(q, k_ref[pl.ds(k0, tk), :], NT, preferred_element_type=F32) * scale
        msk = mask_fn(k0)
        s = jnp.where(msk, s, NEG)
        m_new = jnp.maximum(m_i, jnp.max(s, axis=-1, keepdims=True))
        alpha = jnp.exp(m_i - m_new)
        p = jnp.where(msk, jnp.exp(s - m_new), 0.0)
        l_new = alpha * l_i + jnp.sum(p, axis=-1, keepdims=True)
        acc = alpha * acc + jnp.dot(p.astype(BF16), v_ref[pl.ds(k0, tk), :], preferred_element_type=F32)
        return m_new, l_new, acc

    init = (jnp.full((rows, 1), NEG, F32), jnp.zeros((rows, 1), F32), jnp.zeros((rows, HEAD_DIM), F32))

    def slc_mask(k0):
        expand = jnp.where(e_row == ((k0 + e_col) // SLC_LEN), 1.0, 0.0).astype(BF16)
        selx = jnp.dot(sel, expand, preferred_element_type=F32)
        selx = jnp.concatenate([selx] * hpg, axis=0)
        return (selx > 0.5) & ((k0 + key_lane) <= t_row)

    n_slc_blocks = (t0 + tq + tk - 1) // tk
    _, l_s, acc_s = lax.fori_loop(
        0, n_slc_blocks, lambda kb, c: attend(ks_ref, vs_ref, kb, c, slc_mask), init)
    o_slc = acc_s / l_s

    def win_mask(k0):
        kpos = k0 + key_lane
        return (kpos <= t_row) & (kpos > t_row - WINDOW)

    win_lo = jnp.maximum(t0 - WINDOW + 1, 0) // tk
    _, l_w, acc_w = lax.fori_loop(
        win_lo, n_slc_blocks, lambda kb, c: attend(kw_ref, vw_ref, kb, c, win_mask), init)
    o_win = acc_w / l_w

    sm = sm_ref[...]
    lane = lax.broadcasted_iota(jnp.int32, (tq, SM_W), 1)
    for hh in range(hpg):
        col = GATE_COL0 + (g * hpg + hh) * 3
        rs = slice(hh * tq, (hh + 1) * tq)
        o = (jax.nn.sigmoid(_lane_pick(sm, lane, col)) * o_cmp[rs]
             + jax.nn.sigmoid(_lane_pick(sm, lane, col + 1)) * o_slc[rs]
             + jax.nn.sigmoid(_lane_pick(sm, lane, col + 2)) * o_win[rs])
        o_ref[:, hh * HEAD_DIM:(hh + 1) * HEAD_DIM] = o.astype(o_ref.dtype)


def _nsa_attn(p_nsa, p_sm, cosf, sins, kcmp, vcmp, ksr, vsb, kwr, vwb):
    bsz, seq, _ = p_nsa.shape
    ng = NSA_KV
    tq = NSA_TQ
    ncmp = seq // CMP_STRIDE
    gw = NSA_HPG * HEAD_DIM
    per_group = lambda rows: pl.BlockSpec((None, None, rows, LANES), lambda b, g, t: (b, g, 0, 0))
    return pl.pallas_call(
        functools.partial(_nsa_attn_kernel, seq=seq),
        grid=(bsz, ng, seq // tq),
        in_specs=[pl.BlockSpec((None, tq, gw), lambda b, g, t: (b, t, g)),
                  pl.BlockSpec((tq, LANES), lambda b, g, t: (t, 0)),
                  pl.BlockSpec((tq, LANES), lambda b, g, t: (t, 0)),
                  per_group(ncmp), per_group(ncmp),
                  per_group(seq), per_group(seq), per_group(seq), per_group(seq),
                  pl.BlockSpec((None, tq, SM_W), lambda b, g, t: (b, t, 0))],
        out_specs=pl.BlockSpec((None, tq, gw), lambda b, g, t: (b, t, g)),
        out_shape=jax.ShapeDtypeStruct((bsz, seq, NSA_W), BF16),
        compiler_params=_params(("parallel", "parallel", "arbitrary"), 40),
        name="nsa_attn",
    )(p_nsa, cosf, sins, kcmp, vcmp, ksr, vsb, kwr, vwb, p_sm)


def _rope_tables(seq):
    inv = ROPE_THETA ** (-jnp.arange(0, HEAD_DIM, 2, dtype=F32) / HEAD_DIM)
    ang = jnp.arange(seq, dtype=F32)[:, None] * inv[None, :]
    cos, sin = jnp.cos(ang), jnp.sin(ang)
    return jnp.concatenate([cos, cos], axis=-1), jnp.concatenate([-sin, sin], axis=-1)


def _mixers(p_hg, p_gd, p_nsa, p_sm, lb, hg_norm, conv_w, a_log, dt_bias, gd_norm,
            posk, w1k, w2k, posv, w1v, w2v, cosf, sins):
    o_a = _hgrn(p_hg, lb, hg_norm)
    o_b = _gdn(p_gd, p_sm, conv_w, a_log, dt_bias, gd_norm)
    prep = _nsa_prep(p_nsa, cosf, sins, posk, w1k.astype(BF16), w2k.astype(BF16),
                     posv, w1v.astype(BF16), w2v.astype(BF16))
    o_c = _nsa_attn(p_nsa, p_sm, cosf, sins, *prep)
    return jnp.concatenate([o_a, o_b, o_c], axis=-1)


def kernel(x, attn_norm, w_in, w_out, ffn_norm, w_gate, w_up, w_down, final_norm, hgrn_lb_logits, hgrn_out_norm, gdn_conv, gdn_A_log, gdn_dt_bias, gdn_out_norm, cmp_pos_k, cmp_w1_k, cmp_w2_k, cmp_pos_v, cmp_w1_v, cmp_w2_v):
    bsz, seq, d = x.shape
    depth = w_in.shape[0]
    n = bsz * seq
    cosf, sins = _rope_tables(seq)
    lb_p = jax.nn.softmax(hgrn_lb_logits.astype(F32), axis=0)
    lb_all = jnp.concatenate([jnp.zeros_like(lb_p[:1]), jnp.cumsum(lb_p, axis=0)[:-1]], axis=0)

    c_gd = 4 * HG_W
    c_ab = c_gd + 4 * GD_W
    c_nsa = c_ab + 2 * GD_HEADS
    c_gate = c_nsa + NSA_W + 6 * NSA_KV_W
    n_gate = NSA_HEADS * 3
    ffn_pad = -(-FFN_HIDDEN // 512) * 512 - FFN_HIDDEN

    xf = x.reshape(n, d)
    for l in range(depth):
        wl = w_in[l]
        w_hg = wl[:, :c_gd].astype(BF16)
        w_gd = wl[:, c_gd:c_ab].astype(BF16)
        w_nsa = wl[:, c_nsa:c_gate].astype(BF16)
        w_sm = jnp.concatenate(
            [wl[:, c_ab:c_nsa], wl[:, c_gate:],
             jnp.zeros((d, SM_W - 2 * GD_HEADS - n_gate), F32)], axis=1).astype(BF16)

        h = _rmsnorm(xf, attn_norm[l], BF16)
        p_hg = _matmul(h, w_hg).reshape(bsz, seq, -1)
        p_gd = _matmul(h, w_gd).reshape(bsz, seq, -1)
        p_nsa = _matmul(h, w_nsa).reshape(bsz, seq, -1)
        p_sm = _matmul(h, w_sm).reshape(bsz, seq, -1)
        mix = _mixers(p_hg, p_gd, p_nsa, p_sm, lb_all[l], hgrn_out_norm[l], gdn_conv[l],
                      gdn_A_log[l], gdn_dt_bias[l], gdn_out_norm[l],
                      cmp_pos_k[l], cmp_w1_k[l], cmp_w2_k[l], cmp_pos_v[l], cmp_w1_v[l], cmp_w2_v[l],
                      cosf, sins)
        xf = _matmul(mix.reshape(n, d), w_out[l].astype(BF16), res=xf)

        h = _rmsnorm(xf, ffn_norm[l], BF16)
        wg = jnp.pad(w_gate[l].astype(BF16), ((0, 0), (0, ffn_pad)))
        wu = jnp.pad(w_up[l].astype(BF16), ((0, 0), (0, ffn_pad)))
        wd = jnp.pad(w_down[l].astype(BF16), ((0, ffn_pad), (0, 0)))
        act = _gate_up(h, wg, wu)
        xf = _matmul_ktiled_res(act, wd, xf)
    return _rmsnorm(xf, final_norm, F32).reshape(bsz, seq, d)
```

```python
import functools

import jax
import jax.numpy as jnp
from jax import lax
from jax.experimental import pallas as pl
from jax.experimental.pallas import tpu as pltpu

F32 = jnp.float32
BF16 = jnp.bfloat16
HI = lax.Precision.HIGHEST

D_MODEL = 4096
HEAD_DIM = 128
HG_HEADS = 8
GD_HEADS = 8
CONV_W = 4
NSA_HEADS = 16
NSA_KV = 4
NSA_HPG = NSA_HEADS // NSA_KV
CMP_LEN = 32
CMP_STRIDE = 16
CMP_HIDDEN = 256
SLC_LEN = 64
SLC_TOP = 16
WINDOW = 512
ROPE_THETA = 10000.0
NORM_EPS = 1e-6
LB_FLOOR = 1e-30
NEG = -1e30
FFN_HIDDEN = 11008

HG_W = HG_HEADS * HEAD_DIM
GD_W = GD_HEADS * HEAD_DIM
NSA_W = NSA_HEADS * HEAD_DIM
NSA_KV_W = NSA_KV * HEAD_DIM
SM_W = 128
GATE_COL0 = 2 * GD_HEADS

LANES = 128
MIB = 1 << 20

NT = (((1,), (1,)), ((), ()))
TN = (((0,), (0,)), ((), ()))


def _params(sem, vmem_mib):
    return pltpu.CompilerParams(dimension_semantics=sem, vmem_limit_bytes=vmem_mib * MIB)


def _silu(x):
    return x * jax.nn.sigmoid(x)


def _softplus(x):
    return jnp.maximum(x, 0.0) + jnp.log1p(jnp.exp(-jnp.abs(x)))


def _lane_pick(x, lane_ids, idx):
    return jnp.sum(jnp.where(lane_ids == idx, x, 0.0), axis=-1, keepdims=True)


def _rms_kernel(x_ref, w_ref, o_ref):
    x = x_ref[...]
    ms = jnp.mean(x * x, axis=-1, keepdims=True)
    o_ref[...] = (x * lax.rsqrt(ms + NORM_EPS) * w_ref[...]).astype(o_ref.dtype)


def _rmsnorm(x, w, out_dtype, rows=256):
    n, d = x.shape
    return pl.pallas_call(
        _rms_kernel,
        grid=(n // rows,),
        in_specs=[pl.BlockSpec((rows, d), lambda i: (i, 0)),
                  pl.BlockSpec((1, d), lambda i: (0, 0))],
        out_specs=pl.BlockSpec((rows, d), lambda i: (i, 0)),
        out_shape=jax.ShapeDtypeStruct((n, d), out_dtype),
        compiler_params=_params(("parallel",), 40),
        name="rmsnorm",
    )(x, w.reshape(1, d))


def _mm_kernel(a_ref, b_ref, o_ref):
    o_ref[...] = jnp.dot(a_ref[...], b_ref[...].astype(BF16), preferred_element_type=F32).astype(o_ref.dtype)


def _mm_res_kernel(a_ref, b_ref, r_ref, o_ref):
    o_ref[...] = r_ref[...] + jnp.dot(a_ref[...], b_ref[...].astype(BF16), preferred_element_type=F32)


def _mm_acc_res_kernel(a_ref, b_ref, r_ref, o_ref, acc_ref, *, nk):
    k = pl.program_id(2)

    @pl.when(k == 0)
    def _():
        acc_ref[...] = r_ref[...]

    acc_ref[...] += jnp.dot(a_ref[...], b_ref[...], preferred_element_type=F32)

    @pl.when(k == nk - 1)
    def _():
        o_ref[...] = acc_ref[...]


def _matmul(a, w, layer=0, col0=0, n=None, res=None, tm=1024, tn=512, out_dtype=F32):
    m, k = a.shape
    n = w.shape[2] if n is None else n
    tn = min(tn, n)
    cb = col0 // tn
    in_specs = [pl.BlockSpec((tm, k), lambda i, j: (i, 0)),
                pl.BlockSpec((None, k, tn), lambda i, j: (layer, 0, cb + j))]
    args = [a, w]
    kern = _mm_kernel
    if res is not None:
        in_specs.append(pl.BlockSpec((tm, tn), lambda i, j: (i, j)))
        args.append(res)
        kern = _mm_res_kernel
    return pl.pallas_call(
        kern,
        grid=(m // tm, n // tn),
        in_specs=in_specs,
        out_specs=pl.BlockSpec((tm, tn), lambda i, j: (i, j)),
        out_shape=jax.ShapeDtypeStruct((m, n), out_dtype),
        compiler_params=_params(("parallel", "parallel"), 56),
        name="matmul",
    )(*args)


def _matmul_ktiled_res(a, b, layer, res, tm=1024, tn=512, tk=5504):
    m, k = a.shape
    n = b.shape[2]
    nk = k // tk
    return pl.pallas_call(
        functools.partial(_mm_acc_res_kernel, nk=nk),
        grid=(m // tm, n // tn, nk),
        in_specs=[pl.BlockSpec((tm, tk), lambda i, j, kk: (i, kk)),
                  pl.BlockSpec((None, tk, tn), lambda i, j, kk: (layer, kk, j)),
                  pl.BlockSpec((tm, tn), lambda i, j, kk: (i, j))],
        out_specs=pl.BlockSpec((tm, tn), lambda i, j, kk: (i, j)),
        out_shape=jax.ShapeDtypeStruct((m, n), F32),
        scratch_shapes=[pltpu.VMEM((tm, tn), F32)],
        compiler_params=_params(("parallel", "parallel", "arbitrary"), 48),
        name="matmul_down",
    )(a, b, res)


def _gate_up_kernel(a_ref, wg_ref, wu_ref, o_ref):
    a = a_ref[...]
    g = jnp.dot(a, wg_ref[...].astype(BF16), preferred_element_type=F32)
    u = jnp.dot(a, wu_ref[...].astype(BF16), preferred_element_type=F32)
    o_ref[...] = (_silu(g) * u).astype(o_ref.dtype)


def _gate_up(a, wg, wu, layer, tm=1024, tn=256):
    m, k = a.shape
    n = wg.shape[2]
    return pl.pallas_call(
        _gate_up_kernel,
        grid=(m // tm, n // tn),
        in_specs=[pl.BlockSpec((tm, k), lambda i, j: (i, 0)),
                  pl.BlockSpec((None, k, tn), lambda i, j: (layer, 0, j)),
                  pl.BlockSpec((None, k, tn), lambda i, j: (layer, 0, j))],
        out_specs=pl.BlockSpec((tm, tn), lambda i, j: (i, j)),
        out_shape=jax.ShapeDtypeStruct((m, n), BF16),
        compiler_params=_params(("parallel", "parallel"), 56),
        name="ffn_gate_up",
    )(a, wg, wu)


HG_ROWS = 64
HG_SUB = 16
HG_GROUP = 2


def _hgrn_kernel(q_ref, f_ref, i_ref, g_ref, lb_ref, nw_ref, o_ref, *, seq):
    rows, sub = HG_ROWS, HG_SUB
    row_id = lax.broadcasted_iota(jnp.int32, (rows, LANES), 0)
    sub_id = row_id & (sub - 1)
    r2 = lax.broadcasted_iota(jnp.int32, (rows, rows), 0)
    c2 = lax.broadcasted_iota(jnp.int32, (rows, rows), 1)
    tri = jnp.where(((r2 // sub) == (c2 // sub)) & (c2 <= r2), 1.0, 0.0).astype(BF16)
    log_lb = jnp.log(jnp.maximum(lb_ref[...], LB_FLOOR))
    nw = nw_ref[...]
    nsub = rows // sub

    def features(c):
        r0 = pl.multiple_of(c * rows, rows)
        q = _silu(q_ref[pl.ds(r0, rows), :])
        fp = f_ref[pl.ds(r0, rows), :]
        v = i_ref[pl.ds(r0, rows), :]
        logf = jnp.minimum(-_softplus(-fp) + _softplus(log_lb - fp), 0.0)
        return q, 1.0 - jnp.exp(logf), v, logf

    def step(i, state):
        cs = [i * HG_GROUP + j for j in range(HG_GROUP)]
        feats = [features(c) for c in cs]
        bs = [_dot_exact_lhs(tri, f[3]) for f in feats]
        accs = [jnp.sum(q * kf, axis=-1, keepdims=True) * v for q, kf, v, _ in feats]
        for d in range(1, sub):
            prods = [jnp.where(sub_id >= d,
                               q * pltpu.roll(kf, d, 0) * jnp.exp(jnp.minimum(b - pltpu.roll(b, d, 0), 0.0)), 0.0)
                     for (q, kf, _, _), b in zip(feats, bs)]
            accs = [acc + jnp.sum(p, axis=-1, keepdims=True) * pltpu.roll(f[2], d, 0)
                    for acc, p, f in zip(accs, prods, feats)]

        pieces = []
        for (q, kf, v, _), b in zip(feats, bs):
            for j in range(nsub):
                sl = slice(j * sub, (j + 1) * sub)
                bj = b[sl]
                bl = bj[sub - 1:sub]
                kh = (kf[sl] * jnp.exp(bl - bj)).astype(BF16)
                vk = lax.dot_general(v[sl].astype(BF16), kh, TN, preferred_element_type=F32)
                pieces.append(((q[sl] * jnp.exp(bj)).astype(BF16), jnp.exp(bl), vk))
        inter = []
        for qt, decay, vk in pieces:
            inter.append(lax.dot_general(qt, state.astype(BF16), NT, preferred_element_type=F32))
            state = decay * state + vk

        for n, c in enumerate(cs):
            r0 = pl.multiple_of(c * rows, rows)
            o = accs[n] + jnp.concatenate(inter[n * nsub:(n + 1) * nsub], axis=0)
            ms = jnp.mean(o * o, axis=-1, keepdims=True)
            y = o * lax.rsqrt(ms + NORM_EPS) * nw
            y = y * _silu(g_ref[pl.ds(r0, rows), :])
            o_ref[pl.ds(r0, rows), :] = y.astype(o_ref.dtype)
        return state

    lax.fori_loop(0, seq // (rows * HG_GROUP), step, jnp.zeros((LANES, LANES), F32))


def _hgrn(p_hg, lb, norm_w):
    bsz, seq, _ = p_hg.shape
    nh = HG_HEADS

    def part(k):
        return pl.BlockSpec((None, seq, LANES), lambda b, h, k=k: (b, 0, h + nh * k))

    return pl.pallas_call(
        functools.partial(_hgrn_kernel, seq=seq),
        grid=(bsz, nh),
        in_specs=[part(0), part(1), part(2), part(3),
                  pl.BlockSpec((1, LANES), lambda b, h: (0, h)),
                  pl.BlockSpec((1, LANES), lambda b, h: (0, 0))],
        out_specs=pl.BlockSpec((None, seq, LANES), lambda b, h: (b, 0, h)),
        out_shape=jax.ShapeDtypeStruct((bsz, seq, HG_W), BF16),
        compiler_params=_params(("parallel", "parallel"), 32),
        name="hgrn2",
    )(p_hg, p_hg, p_hg, p_hg, lb.reshape(1, HG_W), norm_w.reshape(1, LANES))


GD_CHUNK = 64
GD_GROUP = 8


def _split2(x):
    hi = x.astype(BF16)
    return hi, (x - hi.astype(F32)).astype(BF16)


def _dot_x3(a, b):
    ah, al = _split2(a)
    bh, bl = _split2(b)
    d = lambda x, y: jnp.dot(x, y, preferred_element_type=F32)
    return d(ah, bh) + (d(ah, bl) + d(al, bh))


def _dot_exact_lhs(a, b):
    hi = b.astype(BF16)
    r = b - hi.astype(F32)
    mid = r.astype(BF16)
    lo = (r - mid.astype(F32)).astype(BF16)
    d = lambda y: jnp.dot(a, y, preferred_element_type=F32)
    return d(hi) + (d(mid) + d(lo))


def _gdn_kernel(q_ref, k_ref, v_ref, z_ref, sm_ref, cq_ref, ck_ref, cv_ref, al_ref, dtb_ref, nw_ref,
                o_ref, am_s, bm_s, qp_s, op_s, dl_s, *, seq):
    ch = GD_CHUNK
    nchunks = seq // ch
    h = pl.program_id(1)
    lane1 = lax.broadcasted_iota(jnp.int32, (1, LANES), 1)
    lane = lax.broadcasted_iota(jnp.int32, (ch, LANES), 1)
    neg_a = -jnp.exp(_lane_pick(al_ref[...], lane1, h))
    dt_bias = _lane_pick(dtb_ref[...], lane1, h)
    r2 = lax.broadcasted_iota(jnp.int32, (ch, ch), 0)
    c2 = lax.broadcasted_iota(jnp.int32, (ch, ch), 1)
    incl = c2 <= r2
    strict = c2 < r2
    tri = jnp.where(incl, 1.0, 0.0).astype(BF16)
    eye = jnp.where(c2 == r2, 1.0, 0.0).astype(F32)
    nw = nw_ref[...]

    def conv_silu(ref, w_ref, r0, c):
        cur = ref[pl.ds(r0, ch), :]
        prev = ref[pl.ds(jnp.maximum(r0 - 8, 0), 8), :]
        prev = jnp.where(c > 0, prev, 0.0)
        x = jnp.concatenate([prev, cur], axis=0)
        w = w_ref[...]
        y = cur * w[CONV_W - 1:CONV_W]
        for s in range(1, CONV_W):
            y = y + pltpu.roll(x, s, 0)[8:] * w[CONV_W - 1 - s:CONV_W - s]
        return _silu(y)

    def features(c):
        r0 = pl.multiple_of(c * ch, ch)
        qf = conv_silu(q_ref, cq_ref, r0, c)
        kf = conv_silu(k_ref, ck_ref, r0, c)
        vf = conv_silu(v_ref, cv_ref, r0, c)
        qf = qf * lax.rsqrt(jnp.sum(qf * qf, axis=-1, keepdims=True) + NORM_EPS) * (HEAD_DIM ** -0.5)
        kf = kf * lax.rsqrt(jnp.sum(kf * kf, axis=-1, keepdims=True) + NORM_EPS)
        sm = sm_ref[pl.ds(r0, ch), :]
        a = _lane_pick(sm, lane, h)
        beta = jax.nn.sigmoid(_lane_pick(sm, lane, h + GD_HEADS))
        g = neg_a * _softplus(a + dt_bias)
        return qf, kf, vf, beta, jnp.broadcast_to(g, (ch, LANES))

    def prepare(i, carry):
        cs = [i * GD_GROUP + j for j in range(GD_GROUP)]
        feats = [features(c) for c in cs]
        gcs = [_dot_exact_lhs(tri, f[4]) for f in feats]
        decays = [jnp.where(incl, jnp.exp(jnp.minimum(gc[:, :ch] - jnp.transpose(gc)[:ch, :], 0.0)), 0.0)
                  for gc in gcs]
        kbs = [f[1].astype(BF16) for f in feats]
        kks = [lax.dot_general(kb, kb, NT, preferred_element_type=F32) for kb in kbs]
        qks = [lax.dot_general(f[0].astype(BF16), kb, NT, preferred_element_type=F32)
               for f, kb in zip(feats, kbs)]
        pws = [jnp.where(strict, -(f[3] * kk * d), 0.0) for f, kk, d in zip(feats, kks, decays)]
        invs = [eye + pw for pw in pws]
        for _ in range(5):
            pws = [_dot_x3(pw, pw) for pw in pws]
            invs = [inv + _dot_x3(inv, pw) for inv, pw in zip(invs, pws)]
        egcs = [jnp.exp(gc) for gc in gcs]
        wus = [_dot_x3(inv, jnp.concatenate([f[1] * (f[3] * egc), f[2] * f[3]], axis=1)).astype(BF16)
               for inv, f, egc in zip(invs, feats, egcs)]
        qos = [jnp.dot(jnp.where(incl, qk * d, 0.0).astype(BF16), wu, preferred_element_type=F32)
               for qk, d, wu in zip(qks, decays, wus)]
        abs_ = [lax.dot_general((f[1] * jnp.exp(gc[ch - 1:ch] - gc)).astype(BF16), wu, TN,
                                preferred_element_type=F32)
                for f, gc, wu in zip(feats, gcs, wus)]
        for c, f, gc, egc, qo, ab in zip(cs, feats, gcs, egcs, qos, abs_):
            r0 = pl.multiple_of(c * ch, ch)
            m0 = pl.multiple_of(c * LANES, LANES)
            am_s[pl.ds(m0, LANES), :] = ab[:, :LANES].astype(BF16)
            bm_s[pl.ds(m0, LANES), :] = ab[:, LANES:]
            qp_s[pl.ds(r0, ch), :] = (f[0] * egc - qo[:, :LANES]).astype(BF16)
            op_s[pl.ds(r0, ch), :] = qo[:, LANES:]
            dl_s[pl.ds(pl.multiple_of(c * 8, 8), 8), :] = jnp.broadcast_to(jnp.exp(gc[ch - 1:ch]), (8, LANES))
        return carry

    lax.fori_loop(0, nchunks // GD_GROUP, prepare, 0)

    def recur(c, state):
        r0 = pl.multiple_of(c * ch, ch)
        m0 = pl.multiple_of(c * LANES, LANES)
        sb = state.astype(BF16)
        o = jnp.dot(qp_s[pl.ds(r0, ch), :], sb, preferred_element_type=F32) + op_s[pl.ds(r0, ch), :]
        dl = dl_s[pl.ds(pl.multiple_of(c * 8, 8), 1), :]
        state = (dl * state - jnp.dot(am_s[pl.ds(m0, LANES), :], sb, preferred_element_type=F32)
                 + bm_s[pl.ds(m0, LANES), :])
        ms = jnp.mean(o * o, axis=-1, keepdims=True)
        y = o * lax.rsqrt(ms + NORM_EPS) * nw
        y = y * _silu(z_ref[pl.ds(r0, ch), :])
        o_ref[pl.ds(r0, ch), :] = y.astype(o_ref.dtype)
        return state

    lax.fori_loop(0, nchunks, recur, jnp.zeros((LANES, LANES), F32), unroll=2)


def _gdn(p_gd, p_sm, conv_w, a_log, dt_bias, norm_w):
    bsz, seq, _ = p_gd.shape
    nh = GD_HEADS
    nchunks = seq // GD_CHUNK

    def part(k):
        return pl.BlockSpec((None, seq, LANES), lambda b, h, k=k: (b, 0, h + nh * k))

    def cpart(k):
        return pl.BlockSpec((CONV_W, LANES), lambda b, h, k=k: (0, h + nh * k))

    pad = lambda t: jnp.pad(t.astype(F32), (0, LANES - nh)).reshape(1, LANES)
    small = pl.BlockSpec((1, LANES), lambda b, h: (0, 0))
    return pl.pallas_call(
        functools.partial(_gdn_kernel, seq=seq),
        grid=(bsz, nh),
        in_specs=[part(0), part(1), part(2), part(3),
                  pl.BlockSpec((None, seq, SM_W), lambda b, h: (b, 0, 0)),
                  cpart(0), cpart(1), cpart(2), small, small, small],
        out_specs=pl.BlockSpec((None, seq, LANES), lambda b, h: (b, 0, h)),
        out_shape=jax.ShapeDtypeStruct((bsz, seq, GD_W), BF16),
        scratch_shapes=[pltpu.VMEM((nchunks * LANES, LANES), BF16), pltpu.VMEM((nchunks * LANES, LANES), F32),
                        pltpu.VMEM((seq, LANES), BF16), pltpu.VMEM((seq, LANES), F32),
                        pltpu.VMEM((nchunks * 8, LANES), F32)],
        compiler_params=_params(("parallel", "parallel"), 40),
        name="gdn",
    )(p_gd, p_gd, p_gd, p_gd, p_sm, conv_w, conv_w, conv_w, pad(a_log), pad(dt_bias),
      norm_w.reshape(1, LANES))


NSA_TQ = 256
NSA_SQ = 128
NSA_TK = 256
NSA_RC = 64
ROPE_ROWS = 256


def _gelu_tanh(x):
    return 0.5 * x * (1.0 + jnp.tanh(0.7978845608028654 * (x + 0.044715 * x * x * x)))


def _nsa_prep_kernel(kc_ref, vc_ref, ks_ref, vs_ref, kw_ref, vw_ref, cos_ref, sin_ref,
                     posk_ref, w1k_ref, w2k_ref, posv_ref, w1v_ref, w2v_ref,
                     kcmp_ref, vcmp_ref, ksr_ref, vsb_ref, kwr_ref, vwb_ref, kcr_s, *, seq):
    ncmp = seq // CMP_STRIDE
    per = CMP_LEN // CMP_STRIDE

    def rope_rows(i, carry):
        r0 = pl.multiple_of(i * ROPE_ROWS, ROPE_ROWS)
        rs = pl.ds(r0, ROPE_ROWS)
        cosf = cos_ref[rs, :]
        sins = sin_ref[rs, :]

        def rope(x):
            return x * cosf + pltpu.roll(x, HEAD_DIM // 2, 1) * sins

        kcr_s[rs, :] = rope(kc_ref[rs, :])
        ksr_ref[rs, :] = rope(ks_ref[rs, :]).astype(BF16)
        kwr_ref[rs, :] = rope(kw_ref[rs, :]).astype(BF16)
        vsb_ref[rs, :] = vs_ref[rs, :].astype(BF16)
        vwb_ref[rs, :] = vw_ref[rs, :].astype(BF16)
        return carry

    lax.fori_loop(0, seq // ROPE_ROWS, rope_rows, 0)

    def compress(src_ref, pos_ref, w1_ref, w2_ref, out_ref):
        parts = [jnp.zeros((ncmp, CMP_HIDDEN), F32) for _ in range(per)]
        for j in range(CMP_STRIDE):
            xj = src_ref[pl.ds(j, ncmp, stride=CMP_STRIDE), :]
            for p in range(per):
                jj = p * CMP_STRIDE + j
                xp = (xj + pos_ref[jj:jj + 1, :]).astype(BF16)
                parts[p] = parts[p] + jnp.dot(xp, w1_ref[jj * HEAD_DIM:(jj + 1) * HEAD_DIM, :],
                                              preferred_element_type=F32)
        pre = parts[0]
        for p in range(1, per):
            pre = pre + pltpu.roll(parts[p], ncmp - p, 0)
        hid = _gelu_tanh(pre)
        out_ref[...] = jnp.dot(hid.astype(BF16), w2_ref[...], preferred_element_type=F32).astype(BF16)

    compress(kcr_s, posk_ref, w1k_ref, w2k_ref, kcmp_ref)
    compress(vc_ref, posv_ref, w1v_ref, w2v_ref, vcmp_ref)


def _nsa_prep(p_nsa, cosf, sins, posk, w1k, w2k, posv, w1v, w2v):
    bsz, seq, _ = p_nsa.shape
    ng = NSA_KV
    ncmp = seq // CMP_STRIDE
    q_blocks = NSA_W // LANES

    def part(k):
        return pl.BlockSpec((None, seq, LANES), lambda b, g, k=k: (b, 0, q_blocks + ng * k + g))

    def whole(shape):
        return pl.BlockSpec(shape, lambda b, g: (0,) * len(shape))

    seq_out = pl.BlockSpec((None, None, seq, LANES), lambda b, g: (b, g, 0, 0))
    cmp_out = pl.BlockSpec((None, None, ncmp, LANES), lambda b, g: (b, g, 0, 0))
    seq_shape = jax.ShapeDtypeStruct((bsz, ng, seq, LANES), BF16)
    cmp_shape = jax.ShapeDtypeStruct((bsz, ng, ncmp, LANES), BF16)
    return pl.pallas_call(
        functools.partial(_nsa_prep_kernel, seq=seq),
        grid=(bsz, ng),
        in_specs=[part(0), part(1), part(2), part(3), part(4), part(5),
                  whole((seq, LANES)), whole((seq, LANES)),
                  whole((CMP_LEN, HEAD_DIM)), whole((CMP_LEN * HEAD_DIM, CMP_HIDDEN)), whole((CMP_HIDDEN, HEAD_DIM)),
                  whole((CMP_LEN, HEAD_DIM)), whole((CMP_LEN * HEAD_DIM, CMP_HIDDEN)), whole((CMP_HIDDEN, HEAD_DIM))],
        out_specs=[cmp_out, cmp_out, seq_out, seq_out, seq_out, seq_out],
        out_shape=[cmp_shape, cmp_shape, seq_shape, seq_shape, seq_shape, seq_shape],
        scratch_shapes=[pltpu.VMEM((seq, LANES), F32)],
        compiler_params=_params(("parallel", "parallel"), 48),
        name="nsa_prep",
    )(p_nsa, p_nsa, p_nsa, p_nsa, p_nsa, p_nsa, cosf, sins, posk, w1k, w2k, posv, w1v, w2v)


def _nsa_attn_kernel(q_ref, cos_ref, sin_ref, kcmp_ref, vcmp_ref, ks_ref, vs_ref, kw_ref, vw_ref,
                     sm_ref, o_ref, *, seq):
    tq, sq, tk, hpg = NSA_TQ, NSA_SQ, NSA_TK, NSA_HPG
    streams = range(tq // sq)
    rows = hpg * sq
    scale = HEAD_DIM ** -0.5
    g = pl.program_id(1)
    t0 = pl.program_id(2) * tq
    ncmp = seq // CMP_STRIDE
    nslc = seq // SLC_LEN
    kcmp = kcmp_ref[...]
    vcmp = vcmp_ref[...]

    def rope_q(st):
        rs = slice(st * sq, (st + 1) * sq)
        cosf = cos_ref[rs, :] * scale
        sins = sin_ref[rs, :] * scale
        heads = []
        for hh in range(hpg):
            x = q_ref[rs, hh * HEAD_DIM:(hh + 1) * HEAD_DIM]
            heads.append((x * cosf + pltpu.roll(x, HEAD_DIM // 2, 1) * sins).astype(BF16))
        return jnp.concatenate(heads, axis=0)

    qs = [rope_q(st) for st in streams]
    t_qs = [t0 + st * sq + lax.broadcasted_iota(jnp.int32, (sq, 1), 0) for st in streams]
    t_ls = [t0 + st * sq + lax.broadcasted_iota(jnp.int32, (1, sq), 1) for st in streams]

    n_lane = lax.broadcasted_iota(jnp.int32, (1, ncmp), 1)
    cmasks = [((n_lane * CMP_STRIDE + (CMP_LEN - 1)) <= t_q)[None] for t_q in t_qs]
    ss = [lax.dot_general(q, kcmp, NT, preferred_element_type=F32).reshape(hpg, sq, ncmp) for q in qs]
    ss = [jnp.where(cm, s, NEG) for cm, s in zip(cmasks, ss)]
    es = [jnp.exp(s - jnp.max(s, axis=-1, keepdims=True)) for s in ss]
    p_cmps = [jnp.where(cm, e / jnp.sum(e, axis=-1, keepdims=True), 0.0) for cm, e in zip(cmasks, es)]
    o_cmps = [jnp.dot(p.reshape(rows, ncmp).astype(BF16), vcmp, preferred_element_type=F32) for p in p_cmps]

    j_i = lax.broadcasted_iota(jnp.int32, (nslc, ncmp), 0)
    n_i = lax.broadcasted_iota(jnp.int32, (nslc, ncmp), 1)
    ovl = (jnp.minimum(n_i * CMP_STRIDE + CMP_LEN, j_i * SLC_LEN + SLC_LEN)
           - jnp.maximum(n_i * CMP_STRIDE, j_i * SLC_LEN))
    ovl = jnp.maximum(ovl, 0).astype(F32) * (1.0 / CMP_LEN)
    imps = [lax.dot_general(ovl, jnp.sum(p, axis=0), NT, precision=HI, preferred_element_type=F32)
            for p in p_cmps]
    blk = lax.broadcasted_iota(jnp.int32, (nslc, 1), 0)
    imps = [jnp.where((blk == 0) | (blk == t_l // SLC_LEN) | (blk == t_l // SLC_LEN - 1), jnp.inf, imp)
            for imp, t_l in zip(imps, t_ls)]
    imps = [jnp.where(blk * SLC_LEN <= t_l, imp, -jnp.inf) for imp, t_l in zip(imps, t_ls)]
    ranks = [jnp.zeros((nslc, sq), F32) for _ in streams]
    for i in range(nslc):
        ranks = [rank + jnp.where((imp[i:i + 1, :] > imp) | ((imp[i:i + 1, :] == imp) & (blk > i)), 1.0, 0.0)
                 for rank, imp in zip(ranks, imps)]
    sels = [jnp.transpose(jnp.concatenate(
        [jnp.where(rank < float(min(SLC_TOP, nslc)), 1.0, 0.0), jnp.zeros((LANES - nslc, sq), F32)],
        axis=0)).astype(BF16) for rank in ranks]

    key_lane = lax.broadcasted_iota(jnp.int32, (1, tk), 1)
    e_row = lax.broadcasted_iota(jnp.int32, (LANES, tk), 0)
    e_col = lax.broadcasted_iota(jnp.int32, (LANES, tk), 1)

    def attend(k_ref, v_ref, kb, carry, bias_fn):
        k0 = pl.multiple_of(kb * tk, tk)
        kblk = k_ref[pl.ds(k0, tk), :]
        vblk = v_ref[pl.ds(k0, tk), :]
        biases = bias_fn(k0)
        ss = [lax.dot_general(q, kblk, NT, preferred_element_type=F32) for q in qs]
        ss = [s.reshape(hpg, sq, tk) for s in ss]
        biases = [bias[None] for bias in biases]
        m_news = [jnp.maximum(c[0], jnp.max(s + bias, axis=-1, keepdims=True).reshape(rows, 1))
                  for c, s, bias in zip(carry, ss, biases)]
        alphas = [jnp.exp(c[0] - m) for c, m in zip(carry, m_news)]
        ps = [jnp.exp((s - m.reshape(hpg, sq, 1)) + bias).reshape(rows, tk)
              for s, m, bias in zip(ss, m_news, biases)]
        ls = [a * c[1] + jnp.sum(p, axis=-1, keepdims=True) for a, c, p in zip(alphas, carry, ps)]
        accs = [a * c[2] + jnp.dot(p.astype(BF16), vblk, preferred_element_type=F32)
                for a, c, p in zip(alphas, carry, ps)]
        return tuple(zip(m_news, ls, accs))

    init = tuple((jnp.full((rows, 1), NEG, F32), jnp.zeros((rows, 1), F32), jnp.zeros((rows, HEAD_DIM), F32))
                 for _ in streams)

    def slc_bias(k0):
        expand = jnp.where(e_row == ((k0 + e_col) // SLC_LEN), 1.0, 0.0).astype(BF16)
        selxs = [jnp.dot(sel, expand, preferred_element_type=F32) for sel in sels]
        return [jnp.where((selx > 0.5) & ((k0 + key_lane) <= t_q), 0.0, NEG) for selx, t_q in zip(selxs, t_qs)]

    n_blocks = (t0 + tq + tk - 1) // tk
    slc = lax.fori_loop(0, n_blocks, lambda kb, c: attend(ks_ref, vs_ref, kb, c, slc_bias), init)
    o_slcs = [acc * (1.0 / l) for _, l, acc in slc]

    def win_bias(k0):
        kpos = k0 + key_lane
        return [jnp.where((kpos <= t_q) & (kpos > t_q - WINDOW), 0.0, NEG) for t_q in t_qs]

    win_lo = jnp.maximum(t0 - WINDOW + 1, 0) // tk
    win = lax.fori_loop(win_lo, n_blocks, lambda kb, c: attend(kw_ref, vw_ref, kb, c, win_bias), init)
    o_wins = [acc * (1.0 / l) for _, l, acc in win]

    lane = lax.broadcasted_iota(jnp.int32, (sq, SM_W), 1)
    for st in streams:
        rs = slice(st * sq, (st + 1) * sq)
        sm = sm_ref[rs, :]
        for hh in range(hpg):
            col = GATE_COL0 + (g * hpg + hh) * 3
            hs = slice(hh * sq, (hh + 1) * sq)
            o = (jax.nn.sigmoid(_lane_pick(sm, lane, col)) * o_cmps[st][hs]
                 + jax.nn.sigmoid(_lane_pick(sm, lane, col + 1)) * o_slcs[st][hs]
                 + jax.nn.sigmoid(_lane_pick(sm, lane, col + 2)) * o_wins[st][hs])
            o_ref[rs, hh * HEAD_DIM:(hh + 1) * HEAD_DIM] = o.astype(o_ref.dtype)


def _nsa_attn(p_nsa, p_sm, cosf, sins, kcmp, vcmp, ksr, vsb, kwr, vwb):
    bsz, seq, _ = p_nsa.shape
    ng = NSA_KV
    tq = NSA_TQ
    ncmp = seq // CMP_STRIDE
    gw = NSA_HPG * HEAD_DIM
    per_group = lambda rows: pl.BlockSpec((None, None, rows, LANES), lambda b, g, t: (b, g, 0, 0))
    return pl.pallas_call(
        functools.partial(_nsa_attn_kernel, seq=seq),
        grid=(bsz, ng, seq // tq),
        in_specs=[pl.BlockSpec((None, tq, gw), lambda b, g, t: (b, t, g)),
                  pl.BlockSpec((tq, LANES), lambda b, g, t: (t, 0)),
                  pl.BlockSpec((tq, LANES), lambda b, g, t: (t, 0)),
                  per_group(ncmp), per_group(ncmp),
                  per_group(seq), per_group(seq), per_group(seq), per_group(seq),
                  pl.BlockSpec((None, tq, SM_W), lambda b, g, t: (b, t, 0))],
        out_specs=pl.BlockSpec((None, tq, gw), lambda b, g, t: (b, t, g)),
        out_shape=jax.ShapeDtypeStruct((bsz, seq, NSA_W), BF16),
        compiler_params=_params(("parallel", "parallel", "arbitrary"), 40),
        name="nsa_attn",
    )(p_nsa, cosf, sins, kcmp, vcmp, ksr, vsb, kwr, vwb, p_sm)


def _rope_tables(seq):
    inv = ROPE_THETA ** (-jnp.arange(0, HEAD_DIM, 2, dtype=F32) / HEAD_DIM)
    ang = jnp.arange(seq, dtype=F32)[:, None] * inv[None, :]
    cos, sin = jnp.cos(ang), jnp.sin(ang)
    return jnp.concatenate([cos, cos], axis=-1), jnp.concatenate([-sin, sin], axis=-1)


def _mixers(p_hg, p_gd, p_nsa, p_sm, lb, hg_norm, conv_w, a_log, dt_bias, gd_norm,
            posk, w1k, w2k, posv, w1v, w2v, cosf, sins):
    o_a = _hgrn(p_hg, lb, hg_norm)
    o_b = _gdn(p_gd, p_sm, conv_w, a_log, dt_bias, gd_norm)
    prep = _nsa_prep(p_nsa, cosf, sins, posk, w1k.astype(BF16), w2k.astype(BF16),
                     posv, w1v.astype(BF16), w2v.astype(BF16))
    o_c = _nsa_attn(p_nsa, p_sm, cosf, sins, *prep)
    return jnp.concatenate([o_a, o_b, o_c], axis=-1)


def kernel(x, attn_norm, w_in, w_out, ffn_norm, w_gate, w_up, w_down, final_norm, hgrn_lb_logits, hgrn_out_norm, gdn_conv, gdn_A_log, gdn_dt_bias, gdn_out_norm, cmp_pos_k, cmp_w1_k, cmp_w2_k, cmp_pos_v, cmp_w1_v, cmp_w2_v):
    bsz, seq, d = x.shape
    depth = w_in.shape[0]
    n = bsz * seq
    cosf, sins = _rope_tables(seq)
    lb_p = jax.nn.softmax(hgrn_lb_logits.astype(F32), axis=0)
    lb_all = jnp.concatenate([jnp.zeros_like(lb_p[:1]), jnp.cumsum(lb_p, axis=0)[:-1]], axis=0)

    c_gd = 4 * HG_W
    c_ab = c_gd + 4 * GD_W
    c_nsa = c_ab + 2 * GD_HEADS
    c_gate = c_nsa + NSA_W + 6 * NSA_KV_W
    n_gate = NSA_HEADS * 3

    w_nsa = w_in[:, :, c_nsa:c_gate].astype(BF16)
    w_sm = jnp.concatenate(
        [w_in[:, :, c_ab:c_nsa], w_in[:, :, c_gate:],
         jnp.zeros((depth, d, SM_W - 2 * GD_HEADS - n_gate), F32)], axis=2).astype(BF16)
    w_dn = w_down.astype(BF16)

    xf = x.reshape(n, d)
    for l in range(depth):
        h = _rmsnorm(xf, attn_norm[l], BF16)
        p_hg = _matmul(h, w_in, l, 0, c_gd).reshape(bsz, seq, -1)
        p_gd = _matmul(h, w_in, l, c_gd, c_ab - c_gd).reshape(bsz, seq, -1)
        p_nsa = _matmul(h, w_nsa, l).reshape(bsz, seq, -1)
        p_sm = _matmul(h, w_sm, l).reshape(bsz, seq, -1)
        mix = _mixers(p_hg, p_gd, p_nsa, p_sm, lb_all[l], hgrn_out_norm[l], gdn_conv[l],
                      gdn_A_log[l], gdn_dt_bias[l], gdn_out_norm[l],
                      cmp_pos_k[l], cmp_w1_k[l], cmp_w2_k[l], cmp_pos_v[l], cmp_w1_v[l], cmp_w2_v[l],
                      cosf, sins)
        xf = _matmul(mix.reshape(n, d), w_out, l, res=xf)

        h = _rmsnorm(xf, ffn_norm[l], BF16)
        act = _gate_up(h, w_gate, w_up, l)
        xf = _matmul_ktiled_res(act, w_dn, l, xf)
    return _rmsnorm(xf, final_norm, F32).reshape(bsz, seq, d)
```

```python
import functools

import jax
import jax.numpy as jnp
from jax import lax
from jax.experimental import pallas as pl
from jax.experimental.pallas import tpu as pltpu

F32 = jnp.float32
BF16 = jnp.bfloat16
HI = lax.Precision.HIGHEST

D_MODEL = 4096
HEAD_DIM = 128
HG_HEADS = 8
GD_HEADS = 8
CONV_W = 4
NSA_HEADS = 16
NSA_KV = 4
NSA_HPG = NSA_HEADS // NSA_KV
CMP_LEN = 32
CMP_STRIDE = 16
CMP_HIDDEN = 256
SLC_LEN = 64
SLC_TOP = 16
WINDOW = 512
ROPE_THETA = 10000.0
NORM_EPS = 1e-6
LB_FLOOR = 1e-30
NEG = -1e30
LOG2E = 1.4426950408889634
FFN_HIDDEN = 11008

HG_W = HG_HEADS * HEAD_DIM
GD_W = GD_HEADS * HEAD_DIM
NSA_W = NSA_HEADS * HEAD_DIM
NSA_KV_W = NSA_KV * HEAD_DIM
SM_W = 128
GATE_COL0 = 2 * GD_HEADS

LANES = 128
MIB = 1 << 20

NT = (((1,), (1,)), ((), ()))
TN = (((0,), (0,)), ((), ()))


def _params(sem, vmem_mib):
    return pltpu.CompilerParams(dimension_semantics=sem, vmem_limit_bytes=vmem_mib * MIB)


def _silu(x):
    return x * jax.nn.sigmoid(x)


def _softplus(x):
    return jnp.maximum(x, 0.0) + jnp.log1p(jnp.exp(-jnp.abs(x)))


def _lane_pick(x, lane_ids, idx):
    return jnp.sum(jnp.where(lane_ids == idx, x, 0.0), axis=-1, keepdims=True)


def _rms_kernel(x_ref, w_ref, o_ref):
    x = x_ref[...]
    ms = jnp.mean(x * x, axis=-1, keepdims=True)
    o_ref[...] = (x * lax.rsqrt(ms + NORM_EPS) * w_ref[...]).astype(o_ref.dtype)


def _rmsnorm(x, w, out_dtype, rows=256):
    n, d = x.shape
    return pl.pallas_call(
        _rms_kernel,
        grid=(n // rows,),
        in_specs=[pl.BlockSpec((rows, d), lambda i: (i, 0)),
                  pl.BlockSpec((1, d), lambda i: (0, 0))],
        out_specs=pl.BlockSpec((rows, d), lambda i: (i, 0)),
        out_shape=jax.ShapeDtypeStruct((n, d), out_dtype),
        compiler_params=_params(("parallel",), 40),
        name="rmsnorm",
    )(x, w.reshape(1, d))


def _mm_kernel(a_ref, b_ref, o_ref):
    o_ref[...] = jnp.dot(a_ref[...], b_ref[...].astype(BF16), preferred_element_type=F32).astype(o_ref.dtype)


def _mm_res_kernel(a_ref, b_ref, r_ref, o_ref):
    o_ref[...] = r_ref[...] + jnp.dot(a_ref[...], b_ref[...].astype(BF16), preferred_element_type=F32)


def _mm_acc_res_kernel(a_ref, b_ref, r_ref, o_ref, acc_ref, *, nk):
    k = pl.program_id(2)

    @pl.when(k == 0)
    def _():
        acc_ref[...] = r_ref[...]

    acc_ref[...] += jnp.dot(a_ref[...], b_ref[...], preferred_element_type=F32)

    @pl.when(k == nk - 1)
    def _():
        o_ref[...] = acc_ref[...]


def _matmul(a, w, layer=0, col0=0, n=None, res=None, tm=1024, tn=512, out_dtype=F32):
    m, k = a.shape
    n = w.shape[2] if n is None else n
    tn = min(tn, n)
    cb = col0 // tn
    in_specs = [pl.BlockSpec((tm, k), lambda i, j: (i, 0)),
                pl.BlockSpec((None, k, tn), lambda i, j: (layer, 0, cb + j))]
    args = [a, w]
    kern = _mm_kernel
    if res is not None:
        in_specs.append(pl.BlockSpec((tm, tn), lambda i, j: (i, j)))
        args.append(res)
        kern = _mm_res_kernel
    return pl.pallas_call(
        kern,
        grid=(m // tm, n // tn),
        in_specs=in_specs,
        out_specs=pl.BlockSpec((tm, tn), lambda i, j: (i, j)),
        out_shape=jax.ShapeDtypeStruct((m, n), out_dtype),
        compiler_params=_params(("parallel", "parallel"), 56),
        name="matmul",
    )(*args)


def _out_proj_kernel(a_ref, b_ref, c_ref, w_ref, r_ref, o_ref):
    ka, kb = a_ref.shape[1], b_ref.shape[1]
    acc = r_ref[...] + jnp.dot(a_ref[...], w_ref[0:ka, :].astype(BF16), preferred_element_type=F32)
    acc = acc + jnp.dot(b_ref[...], w_ref[ka:ka + kb, :].astype(BF16), preferred_element_type=F32)
    o_ref[...] = acc + jnp.dot(c_ref[...], w_ref[ka + kb:, :].astype(BF16), preferred_element_type=F32)


def _out_proj(parts, w, layer, res, tm=1024, tn=512):
    m, n = res.shape
    k = w.shape[1]
    return pl.pallas_call(
        _out_proj_kernel,
        grid=(m // tm, n // tn),
        in_specs=[pl.BlockSpec((tm, p.shape[1]), lambda i, j: (i, 0)) for p in parts]
        + [pl.BlockSpec((None, k, tn), lambda i, j: (layer, 0, j)),
           pl.BlockSpec((tm, tn), lambda i, j: (i, j))],
        out_specs=pl.BlockSpec((tm, tn), lambda i, j: (i, j)),
        out_shape=jax.ShapeDtypeStruct((m, n), F32),
        compiler_params=_params(("parallel", "parallel"), 56),
        name="out_proj",
    )(*parts, w, res)


def _matmul_ktiled_res(a, b, layer, res, tm=1024, tn=512, tk=5504):
    m, k = a.shape
    n = b.shape[2]
    nk = k // tk
    return pl.pallas_call(
        functools.partial(_mm_acc_res_kernel, nk=nk),
        grid=(m // tm, n // tn, nk),
        in_specs=[pl.BlockSpec((tm, tk), lambda i, j, kk: (i, kk)),
                  pl.BlockSpec((None, tk, tn), lambda i, j, kk: (layer, kk, j)),
                  pl.BlockSpec((tm, tn), lambda i, j, kk: (i, j))],
        out_specs=pl.BlockSpec((tm, tn), lambda i, j, kk: (i, j)),
        out_shape=jax.ShapeDtypeStruct((m, n), F32),
        scratch_shapes=[pltpu.VMEM((tm, tn), F32)],
        compiler_params=_params(("parallel", "parallel", "arbitrary"), 48),
        name="matmul_down",
    )(a, b, res)


def _gate_up_kernel(a_ref, wg_ref, wu_ref, o_ref):
    a = a_ref[...]
    g = jnp.dot(a, wg_ref[...].astype(BF16), preferred_element_type=F32)
    u = jnp.dot(a, wu_ref[...].astype(BF16), preferred_element_type=F32)
    o_ref[...] = (_silu(g) * u).astype(o_ref.dtype)


def _gate_up(a, wg, wu, layer, tm=1024, tn=256):
    m, k = a.shape
    n = wg.shape[2]
    return pl.pallas_call(
        _gate_up_kernel,
        grid=(m // tm, n // tn),
        in_specs=[pl.BlockSpec((tm, k), lambda i, j: (i, 0)),
                  pl.BlockSpec((None, k, tn), lambda i, j: (layer, 0, j)),
                  pl.BlockSpec((None, k, tn), lambda i, j: (layer, 0, j))],
        out_specs=pl.BlockSpec((tm, tn), lambda i, j: (i, j)),
        out_shape=jax.ShapeDtypeStruct((m, n), BF16),
        compiler_params=_params(("parallel", "parallel"), 56),
        name="ffn_gate_up",
    )(a, wg, wu)


HG_ROWS = 64
HG_SUB = 16
HG_GROUP = 2


def _hgrn_kernel(q_ref, f_ref, i_ref, g_ref, lb_ref, nw_ref, o_ref, *, seq):
    rows, sub = HG_ROWS, HG_SUB
    row_id = lax.broadcasted_iota(jnp.int32, (rows, LANES), 0)
    sub_id = row_id & (sub - 1)
    r2 = lax.broadcasted_iota(jnp.int32, (rows, rows), 0)
    c2 = lax.broadcasted_iota(jnp.int32, (rows, rows), 1)
    tri = jnp.where(((r2 // sub) == (c2 // sub)) & (c2 <= r2), 1.0, 0.0).astype(BF16)
    log_lb = jnp.log(jnp.maximum(lb_ref[...], LB_FLOOR))
    nw = nw_ref[...]
    nsub = rows // sub

    def features(c):
        r0 = pl.multiple_of(c * rows, rows)
        q = _silu(q_ref[pl.ds(r0, rows), :])
        fp = f_ref[pl.ds(r0, rows), :]
        v = i_ref[pl.ds(r0, rows), :]
        logf = jnp.minimum(-_softplus(-fp) + _softplus(log_lb - fp), 0.0)
        return q, 1.0 - jnp.exp(logf), v, logf

    def step(i, state):
        cs = [i * HG_GROUP + j for j in range(HG_GROUP)]
        feats = [features(c) for c in cs]
        bs = [_dot_exact_lhs(tri, f[3]) for f in feats]
        accs = [jnp.sum(q * kf, axis=-1, keepdims=True) * v for q, kf, v, _ in feats]
        for d in range(1, sub):
            prods = [jnp.where(sub_id >= d,
                               q * pltpu.roll(kf, d, 0) * jnp.exp(jnp.minimum(b - pltpu.roll(b, d, 0), 0.0)), 0.0)
                     for (q, kf, _, _), b in zip(feats, bs)]
            accs = [acc + jnp.sum(p, axis=-1, keepdims=True) * pltpu.roll(f[2], d, 0)
                    for acc, p, f in zip(accs, prods, feats)]

        pieces = []
        for (q, kf, v, _), b in zip(feats, bs):
            for j in range(nsub):
                sl = slice(j * sub, (j + 1) * sub)
                bj = b[sl]
                bl = bj[sub - 1:sub]
                kh = (kf[sl] * jnp.exp(bl - bj)).astype(BF16)
                vk = lax.dot_general(v[sl].astype(BF16), kh, TN, preferred_element_type=F32)
                pieces.append(((q[sl] * jnp.exp(bj)).astype(BF16), jnp.exp(bl), vk))
        inter = []
        for qt, decay, vk in pieces:
            inter.append(lax.dot_general(qt, state.astype(BF16), NT, preferred_element_type=F32))
            state = decay * state + vk

        for n, c in enumerate(cs):
            r0 = pl.multiple_of(c * rows, rows)
            o = accs[n] + jnp.concatenate(inter[n * nsub:(n + 1) * nsub], axis=0)
            ms = jnp.mean(o * o, axis=-1, keepdims=True)
            y = o * lax.rsqrt(ms + NORM_EPS) * nw
            y = y * _silu(g_ref[pl.ds(r0, rows), :])
            o_ref[pl.ds(r0, rows), :] = y.astype(o_ref.dtype)
        return state

    lax.fori_loop(0, seq // (rows * HG_GROUP), step, jnp.zeros((LANES, LANES), F32))


def _hgrn(p_hg, lb, norm_w):
    bsz, seq, _ = p_hg.shape
    assert seq % (HG_ROWS * HG_GROUP) == 0
    nh = HG_HEADS

    def part(k):
        return pl.BlockSpec((None, seq, LANES), lambda b, h, k=k: (b, 0, h + nh * k))

    return pl.pallas_call(
        functools.partial(_hgrn_kernel, seq=seq),
        grid=(bsz, nh),
        in_specs=[part(0), part(1), part(2), part(3),
                  pl.BlockSpec((1, LANES), lambda b, h: (0, h)),
                  pl.BlockSpec((1, LANES), lambda b, h: (0, 0))],
        out_specs=pl.BlockSpec((None, seq, LANES), lambda b, h: (b, 0, h)),
        out_shape=jax.ShapeDtypeStruct((bsz, seq, HG_W), BF16),
        compiler_params=_params(("parallel", "parallel"), 32),
        name="hgrn2",
    )(p_hg, p_hg, p_hg, p_hg, lb.reshape(1, HG_W), norm_w.reshape(1, LANES))


GD_CHUNK = 64
GD_GROUP = 8


def _split2(x):
    hi = x.astype(BF16)
    return hi, (x - hi.astype(F32)).astype(BF16)


def _dot_x3(a, b):
    ah, al = _split2(a)
    bh, bl = _split2(b)
    d = lambda x, y: jnp.dot(x, y, preferred_element_type=F32)
    return d(ah, bh) + (d(ah, bl) + d(al, bh))


def _dot_exact_lhs(a, b):
    hi = b.astype(BF16)
    r = b - hi.astype(F32)
    mid = r.astype(BF16)
    lo = (r - mid.astype(F32)).astype(BF16)
    d = lambda y: jnp.dot(a, y, preferred_element_type=F32)
    return d(hi) + (d(mid) + d(lo))


def _gdn_kernel(q_ref, k_ref, v_ref, z_ref, sm_ref, cq_ref, ck_ref, cv_ref, al_ref, dtb_ref, nw_ref,
                o_ref, am_s, bm_s, qp_s, op_s, dl_s, *, seq):
    ch = GD_CHUNK
    nchunks = seq // ch
    h = pl.program_id(1)
    lane1 = lax.broadcasted_iota(jnp.int32, (1, LANES), 1)
    lane = lax.broadcasted_iota(jnp.int32, (ch, LANES), 1)
    neg_a = -jnp.exp(_lane_pick(al_ref[...], lane1, h))
    dt_bias = _lane_pick(dtb_ref[...], lane1, h)
    r2 = lax.broadcasted_iota(jnp.int32, (ch, ch), 0)
    c2 = lax.broadcasted_iota(jnp.int32, (ch, ch), 1)
    incl = c2 <= r2
    strict = c2 < r2
    tri = jnp.where(incl, 1.0, 0.0).astype(BF16)
    eye = jnp.where(c2 == r2, 1.0, 0.0).astype(F32)
    nw = nw_ref[...]

    def conv_silu(ref, w_ref, r0, c):
        cur = ref[pl.ds(r0, ch), :]
        prev = ref[pl.ds(jnp.maximum(r0 - 8, 0), 8), :]
        prev = jnp.where(c > 0, prev, 0.0)
        x = jnp.concatenate([prev, cur], axis=0)
        w = w_ref[...]
        y = cur * w[CONV_W - 1:CONV_W]
        for s in range(1, CONV_W):
            y = y + pltpu.roll(x, s, 0)[8:] * w[CONV_W - 1 - s:CONV_W - s]
        return _silu(y)

    def features(c):
        r0 = pl.multiple_of(c * ch, ch)
        qf = conv_silu(q_ref, cq_ref, r0, c)
        kf = conv_silu(k_ref, ck_ref, r0, c)
        vf = conv_silu(v_ref, cv_ref, r0, c)
        qf = qf * lax.rsqrt(jnp.sum(qf * qf, axis=-1, keepdims=True) + NORM_EPS) * (HEAD_DIM ** -0.5)
        kf = kf * lax.rsqrt(jnp.sum(kf * kf, axis=-1, keepdims=True) + NORM_EPS)
        sm = sm_ref[pl.ds(r0, ch), :]
        a = _lane_pick(sm, lane, h)
        beta = jax.nn.sigmoid(_lane_pick(sm, lane, h + GD_HEADS))
        g = neg_a * _softplus(a + dt_bias)
        return qf, kf, vf, beta, jnp.broadcast_to(g, (ch, LANES))

    def prepare(i, carry):
        cs = [i * GD_GROUP + j for j in range(GD_GROUP)]
        feats = [features(c) for c in cs]
        gcs = [_dot_exact_lhs(tri, f[4]) for f in feats]
        decays = [jnp.where(incl, jnp.exp(jnp.minimum(gc[:, :ch] - jnp.transpose(gc)[:ch, :], 0.0)), 0.0)
                  for gc in gcs]
        kbs = [f[1].astype(BF16) for f in feats]
        kks = [lax.dot_general(kb, kb, NT, preferred_element_type=F32) for kb in kbs]
        qks = [lax.dot_general(f[0].astype(BF16), kb, NT, preferred_element_type=F32)
               for f, kb in zip(feats, kbs)]
        pws = [jnp.where(strict, -(f[3] * kk * d), 0.0) for f, kk, d in zip(feats, kks, decays)]
        invs = [eye + pw for pw in pws]
        for _ in range(5):
            pws = [_dot_x3(pw, pw) for pw in pws]
            invs = [inv + _dot_x3(inv, pw) for inv, pw in zip(invs, pws)]
        egcs = [jnp.exp(gc) for gc in gcs]
        wus = [_dot_x3(inv, jnp.concatenate([f[1] * (f[3] * egc), f[2] * f[3]], axis=1)).astype(BF16)
               for inv, f, egc in zip(invs, feats, egcs)]
        qos = [jnp.dot(jnp.where(incl, qk * d, 0.0).astype(BF16), wu, preferred_element_type=F32)
               for qk, d, wu in zip(qks, decays, wus)]
        abs_ = [lax.dot_general((f[1] * jnp.exp(gc[ch - 1:ch] - gc)).astype(BF16), wu, TN,
                                preferred_element_type=F32)
                for f, gc, wu in zip(feats, gcs, wus)]
        for c, f, gc, egc, qo, ab in zip(cs, feats, gcs, egcs, qos, abs_):
            r0 = pl.multiple_of(c * ch, ch)
            m0 = pl.multiple_of(c * LANES, LANES)
            am_s[pl.ds(m0, LANES), :] = ab[:, :LANES].astype(BF16)
            bm_s[pl.ds(m0, LANES), :] = ab[:, LANES:]
            qp_s[pl.ds(r0, ch), :] = (f[0] * egc - qo[:, :LANES]).astype(BF16)
            op_s[pl.ds(r0, ch), :] = qo[:, LANES:]
            dl_s[pl.ds(pl.multiple_of(c * 8, 8), 8), :] = jnp.broadcast_to(jnp.exp(gc[ch - 1:ch]), (8, LANES))
        return carry

    lax.fori_loop(0, nchunks // GD_GROUP, prepare, 0)

    def recur(c, state):
        r0 = pl.multiple_of(c * ch, ch)
        m0 = pl.multiple_of(c * LANES, LANES)
        sb = state.astype(BF16)
        o = jnp.dot(qp_s[pl.ds(r0, ch), :], sb, preferred_element_type=F32) + op_s[pl.ds(r0, ch), :]
        dl = dl_s[pl.ds(pl.multiple_of(c * 8, 8), 1), :]
        state = (dl * state - jnp.dot(am_s[pl.ds(m0, LANES), :], sb, preferred_element_type=F32)
                 + bm_s[pl.ds(m0, LANES), :])
        ms = jnp.mean(o * o, axis=-1, keepdims=True)
        y = o * lax.rsqrt(ms + NORM_EPS) * nw
        y = y * _silu(z_ref[pl.ds(r0, ch), :])
        o_ref[pl.ds(r0, ch), :] = y.astype(o_ref.dtype)
        return state

    lax.fori_loop(0, nchunks, recur, jnp.zeros((LANES, LANES), F32), unroll=2)


def _gdn(p_gd, p_sm, conv_w, a_log, dt_bias, norm_w):
    bsz, seq, _ = p_gd.shape
    nh = GD_HEADS
    nchunks = seq // GD_CHUNK
    assert seq % (GD_CHUNK * GD_GROUP) == 0

    def part(k):
        return pl.BlockSpec((None, seq, LANES), lambda b, h, k=k: (b, 0, h + nh * k))

    def cpart(k):
        return pl.BlockSpec((CONV_W, LANES), lambda b, h, k=k: (0, h + nh * k))

    pad = lambda t: jnp.pad(t.astype(F32), (0, LANES - nh)).reshape(1, LANES)
    small = pl.BlockSpec((1, LANES), lambda b, h: (0, 0))
    return pl.pallas_call(
        functools.partial(_gdn_kernel, seq=seq),
        grid=(bsz, nh),
        in_specs=[part(0), part(1), part(2), part(3),
                  pl.BlockSpec((None, seq, SM_W), lambda b, h: (b, 0, 0)),
                  cpart(0), cpart(1), cpart(2), small, small, small],
        out_specs=pl.BlockSpec((None, seq, LANES), lambda b, h: (b, 0, h)),
        out_shape=jax.ShapeDtypeStruct((bsz, seq, GD_W), BF16),
        scratch_shapes=[pltpu.VMEM((nchunks * LANES, LANES), BF16), pltpu.VMEM((nchunks * LANES, LANES), F32),
                        pltpu.VMEM((seq, LANES), BF16), pltpu.VMEM((seq, LANES), F32),
                        pltpu.VMEM((nchunks * 8, LANES), F32)],
        compiler_params=_params(("parallel", "parallel"), 40),
        name="gdn",
    )(p_gd, p_gd, p_gd, p_gd, p_sm, conv_w, conv_w, conv_w, pad(a_log), pad(dt_bias),
      norm_w.reshape(1, LANES))


NSA_TQ = 256
NSA_SQ = 128
NSA_TK = 256
NSA_RC = 64
ROPE_ROWS = 256


def _gelu_tanh(x):
    return 0.5 * x * (1.0 + jnp.tanh(0.7978845608028654 * (x + 0.044715 * x * x * x)))


def _nsa_prep_kernel(kc_ref, vc_ref, ks_ref, vs_ref, kw_ref, vw_ref, cos_ref, sin_ref,
                     posk_ref, w1k_ref, w2k_ref, posv_ref, w1v_ref, w2v_ref,
                     kcmp_ref, vcmp_ref, ksr_ref, vsb_ref, kwr_ref, vwb_ref, kcr_s, *, seq):
    ncmp = seq // CMP_STRIDE
    per = CMP_LEN // CMP_STRIDE

    def rope_rows(i, carry):
        r0 = pl.multiple_of(i * ROPE_ROWS, ROPE_ROWS)
        rs = pl.ds(r0, ROPE_ROWS)
        cosf = cos_ref[rs, :]
        sins = sin_ref[rs, :]

        def rope(x):
            return x * cosf + pltpu.roll(x, HEAD_DIM // 2, 1) * sins

        kcr_s[rs, :] = rope(kc_ref[rs, :])
        ksr_ref[rs, :] = rope(ks_ref[rs, :]).astype(BF16)
        kwr_ref[rs, :] = rope(kw_ref[rs, :]).astype(BF16)
        vsb_ref[:, rs] = jnp.transpose(vs_ref[rs, :]).astype(BF16)
        vwb_ref[:, rs] = jnp.transpose(vw_ref[rs, :]).astype(BF16)
        return carry

    lax.fori_loop(0, seq // ROPE_ROWS, rope_rows, 0)

    def compress(src_ref, pos_ref, w1_ref, w2_ref, out_ref, transpose_out):
        parts = [jnp.zeros((ncmp, CMP_HIDDEN), F32) for _ in range(per)]
        for j in range(CMP_STRIDE):
            xj = src_ref[pl.ds(j, ncmp, stride=CMP_STRIDE), :]
            for p in range(per):
                jj = p * CMP_STRIDE + j
                xp = (xj + pos_ref[jj:jj + 1, :]).astype(BF16)
                parts[p] = parts[p] + jnp.dot(xp, w1_ref[jj * HEAD_DIM:(jj + 1) * HEAD_DIM, :],
                                              preferred_element_type=F32)
        pre = parts[0]
        for p in range(1, per):
            pre = pre + pltpu.roll(parts[p], ncmp - p, 0)
        hid = _gelu_tanh(pre)
        out = jnp.dot(hid.astype(BF16), w2_ref[...], preferred_element_type=F32)
        out_ref[...] = (jnp.transpose(out) if transpose_out else out).astype(BF16)

    compress(kcr_s, posk_ref, w1k_ref, w2k_ref, kcmp_ref, False)
    compress(vc_ref, posv_ref, w1v_ref, w2v_ref, vcmp_ref, True)


def _nsa_prep(p_nsa, cosf, sins, posk, w1k, w2k, posv, w1v, w2v):
    bsz, seq, _ = p_nsa.shape
    ng = NSA_KV
    ncmp = seq // CMP_STRIDE
    q_blocks = NSA_W // LANES

    def part(k):
        return pl.BlockSpec((None, seq, LANES), lambda b, g, k=k: (b, 0, q_blocks + ng * k + g))

    def whole(shape):
        return pl.BlockSpec(shape, lambda b, g: (0,) * len(shape))

    seq_out = pl.BlockSpec((None, None, seq, LANES), lambda b, g: (b, g, 0, 0))
    seq_out_t = pl.BlockSpec((None, None, LANES, seq), lambda b, g: (b, g, 0, 0))
    cmp_out = pl.BlockSpec((None, None, ncmp, LANES), lambda b, g: (b, g, 0, 0))
    seq_shape = jax.ShapeDtypeStruct((bsz, ng, seq, LANES), BF16)
    seq_shape_t = jax.ShapeDtypeStruct((bsz, ng, LANES, seq), BF16)
    cmp_shape = jax.ShapeDtypeStruct((bsz, ng, ncmp, LANES), BF16)
    return pl.pallas_call(
        functools.partial(_nsa_prep_kernel, seq=seq),
        grid=(bsz, ng),
        in_specs=[part(0), part(1), part(2), part(3), part(4), part(5),
                  whole((seq, LANES)), whole((seq, LANES)),
                  whole((CMP_LEN, HEAD_DIM)), whole((CMP_LEN * HEAD_DIM, CMP_HIDDEN)), whole((CMP_HIDDEN, HEAD_DIM)),
                  whole((CMP_LEN, HEAD_DIM)), whole((CMP_LEN * HEAD_DIM, CMP_HIDDEN)), whole((CMP_HIDDEN, HEAD_DIM))],
        out_specs=[cmp_out, cmp_out, seq_out, seq_out_t, seq_out, seq_out_t],
        out_shape=[cmp_shape, cmp_shape, seq_shape, seq_shape_t, seq_shape, seq_shape_t],
        scratch_shapes=[pltpu.VMEM((seq, LANES), F32)],
        compiler_params=_params(("parallel", "parallel"), 48),
        name="nsa_prep",
    )(p_nsa, p_nsa, p_nsa, p_nsa, p_nsa, p_nsa, cosf, sins, posk, w1k, w2k, posv, w1v, w2v)


def _nsa_attn_kernel(q_ref, cos_ref, sin_ref, kcmp_ref, vcmp_ref, ks_ref, vs_ref, kw_ref, vw_ref,
                     sm_ref, o_ref, sel_s, *, seq):
    tq, sq, tk, hpg = NSA_TQ, NSA_SQ, NSA_TK, NSA_HPG
    streams = range(tq // sq)
    cols = hpg * sq
    scale = HEAD_DIM ** -0.5
    g = pl.program_id(1)
    t0 = pl.program_id(2) * tq
    ncmp = seq // CMP_STRIDE
    nslc = seq // SLC_LEN
    kcmp = kcmp_ref[...]
    vcmp_t = vcmp_ref[...]
    heads = [slice(hh * sq, (hh + 1) * sq) for hh in range(hpg)]

    def rope_q_t(st):
        rs = slice(st * sq, (st + 1) * sq)
        cosf = cos_ref[rs, :] * (scale * LOG2E)
        sins = sin_ref[rs, :] * (scale * LOG2E)
        out = []
        for hh in range(hpg):
            x = q_ref[rs, hh * HEAD_DIM:(hh + 1) * HEAD_DIM]
            out.append(jnp.transpose(x * cosf + pltpu.roll(x, HEAD_DIM // 2, 1) * sins).astype(BF16))
        return jnp.concatenate(out, axis=1)

    qts = [rope_q_t(st) for st in streams]
    t_ls = [t0 + st * sq + lax.broadcasted_iota(jnp.int32, (1, sq), 1) for st in streams]

    n_sub = lax.broadcasted_iota(jnp.int32, (ncmp, 1), 0)
    cmasks = [(n_sub * CMP_STRIDE + (CMP_LEN - 1)) <= t_l for t_l in t_ls]
    ss = [jnp.dot(kcmp, qt, preferred_element_type=F32) for qt in qts]
    ss = [[jnp.where(cm, s[:, hd], NEG) for hd in heads] for cm, s in zip(cmasks, ss)]
    es = [[jnp.exp2(s - jnp.max(s, axis=0, keepdims=True)) for s in sh] for sh in ss]
    p_cmps = [[jnp.where(cm, e * (1.0 / jnp.sum(e, axis=0, keepdims=True)), 0.0) for e in eh]
              for cm, eh in zip(cmasks, es)]
    o_cmps = [jnp.dot(vcmp_t, jnp.concatenate(ph, axis=1).astype(BF16), preferred_element_type=F32)
              for ph in p_cmps]

    j_i = lax.broadcasted_iota(jnp.int32, (nslc, ncmp), 0)
    n_i = lax.broadcasted_iota(jnp.int32, (nslc, ncmp), 1)
    ovl = (jnp.minimum(n_i * CMP_STRIDE + CMP_LEN, j_i * SLC_LEN + SLC_LEN)
           - jnp.maximum(n_i * CMP_STRIDE, j_i * SLC_LEN))
    ovl = jnp.maximum(ovl, 0).astype(F32) * (1.0 / CMP_LEN)
    imps = [jnp.dot(ovl, functools.reduce(lambda a, b: a + b, ph), precision=HI, preferred_element_type=F32)
            for ph in p_cmps]
    blk = lax.broadcasted_iota(jnp.int32, (nslc, 1), 0)
    imps = [jnp.where((blk == 0) | (blk == t_l // SLC_LEN) | (blk == t_l // SLC_LEN - 1), jnp.inf, imp)
            for imp, t_l in zip(imps, t_ls)]
    imps = [jnp.where(blk * SLC_LEN <= t_l, imp, -jnp.inf) for imp, t_l in zip(imps, t_ls)]
    ranks = [jnp.zeros((nslc, sq), F32) for _ in streams]
    for i in range(nslc):
        ranks = [rank + jnp.where((imp[i:i + 1, :] > imp) | ((imp[i:i + 1, :] == imp) & (blk > i)), 1.0, 0.0)
                 for rank, imp in zip(ranks, imps)]
    for st, rank in zip(streams, ranks):
        sel_s[st * nslc:(st + 1) * nslc, :] = jnp.where(rank < float(min(SLC_TOP, nslc)), 1.0, 0.0)

    key_sub = lax.broadcasted_iota(jnp.int32, (tk, 1), 0)
    slc_sub = lax.broadcasted_iota(jnp.int32, (SLC_LEN, 1), 0)

    units = [(st, hd) for st in streams for hd in heads]

    def attend(kb, carries, jobs):
        k0 = pl.multiple_of(kb * tk, tk)
        work = []
        for j, (k_ref, vt_ref, bias_fn) in enumerate(jobs):
            kblk = k_ref[pl.ds(k0, tk), :]
            vblk_t = vt_ref[:, pl.ds(k0, tk)]
            biases = bias_fn(k0)
            work += [(j, u, kblk, vblk_t, biases[st], qts[st][:, hd]) for u, (st, hd) in enumerate(units)]

        def scores(w):
            return jnp.dot(work[w][2], work[w][5], preferred_element_type=F32)

        out = [[None] * len(units) for _ in jobs]
        s_next = scores(0)
        for w, (j, u, _, vblk_t, bias, _) in enumerate(work):
            sh = s_next + bias
            if w + 1 < len(work):
                s_next = scores(w + 1)
            m_i, l_i, acc = carries[j][u]
            m_new = jnp.maximum(m_i, jnp.max(sh, axis=0, keepdims=True))
            alpha = jnp.exp2(m_i - m_new)
            p = jnp.exp2(sh - m_new)
            l_new = alpha * l_i + jnp.sum(p, axis=0, keepdims=True)
            pv = jnp.dot(vblk_t, p.astype(BF16), preferred_element_type=F32)
            out[j][u] = (m_new, l_new, alpha * acc + pv)
        return tuple(tuple(o) for o in out)

    init = tuple((jnp.full((1, sq), NEG, F32), jnp.zeros((1, sq), F32), jnp.zeros((HEAD_DIM, sq), F32))
                 for _ in units)

    def slc_bias(k0):
        out = []
        for st, t_l in zip(streams, t_ls):
            pieces = []
            for i in range(tk // SLC_LEN):
                picked = sel_s[pl.ds(st * nslc + k0 // SLC_LEN + i, 1), :]
                kpos = k0 + i * SLC_LEN + slc_sub
                pieces.append(jnp.where((picked > 0.5) & (kpos <= t_l), 0.0, NEG))
            out.append(jnp.concatenate(pieces, axis=0))
        return out

    def win_bias(k0):
        kpos = k0 + key_sub
        return [jnp.where((kpos <= t_l) & (kpos > t_l - WINDOW), 0.0, NEG) for t_l in t_ls]

    n_blocks = (t0 + tq + tk - 1) // tk
    win_lo = jnp.maximum(t0 - WINDOW + 1, 0) // tk
    slc_job = (ks_ref, vs_ref, slc_bias)
    win_job = (kw_ref, vw_ref, win_bias)
    (slc,) = lax.fori_loop(0, win_lo, lambda kb, c: attend(kb, c, [slc_job]), (init,))
    slc, win = lax.fori_loop(win_lo, n_blocks, lambda kb, c: attend(kb, c, [slc_job, win_job]), (slc, init))
    o_slcs = [acc * (1.0 / l) for _, l, acc in slc]
    o_wins = [acc * (1.0 / l) for _, l, acc in win]

    gate_row = lax.broadcasted_iota(jnp.int32, (SM_W, sq), 0)
    for st in streams:
        rs = slice(st * sq, (st + 1) * sq)
        sm_t = jnp.transpose(sm_ref[rs, :])

        def gate(col):
            return jax.nn.sigmoid(jnp.sum(jnp.where(gate_row == col, sm_t, 0.0), axis=0, keepdims=True))

        for hh, hd in enumerate(heads):
            col = GATE_COL0 + (g * hpg + hh) * 3
            u = st * hpg + hh
            o_t = (gate(col) * o_cmps[st][:, hd] + gate(col + 1) * o_slcs[u]
                   + gate(col + 2) * o_wins[u])
            o_ref[rs, hh * HEAD_DIM:(hh + 1) * HEAD_DIM] = jnp.transpose(o_t).astype(o_ref.dtype)


def _nsa_attn(p_nsa, p_sm, cosf, sins, kcmp, vcmp, ksr, vsb, kwr, vwb):
    bsz, seq, _ = p_nsa.shape
    ng = NSA_KV
    tq = NSA_TQ
    assert seq % tq == 0 and seq // CMP_STRIDE == LANES
    ncmp = seq // CMP_STRIDE
    gw = NSA_HPG * HEAD_DIM
    per_group = lambda rows: pl.BlockSpec((None, None, rows, LANES), lambda b, g, t: (b, g, 0, 0))
    per_group_t = pl.BlockSpec((None, None, LANES, seq), lambda b, g, t: (b, g, 0, 0))
    return pl.pallas_call(
        functools.partial(_nsa_attn_kernel, seq=seq),
        grid=(bsz, ng, seq // tq),
        in_specs=[pl.BlockSpec((None, tq, gw), lambda b, g, t: (b, t, g)),
                  pl.BlockSpec((tq, LANES), lambda b, g, t: (t, 0)),
                  pl.BlockSpec((tq, LANES), lambda b, g, t: (t, 0)),
                  per_group(ncmp), per_group(ncmp),
                  per_group(seq), per_group_t, per_group(seq), per_group_t,
                  pl.BlockSpec((None, tq, SM_W), lambda b, g, t: (b, t, 0))],
        out_specs=pl.BlockSpec((None, tq, gw), lambda b, g, t: (b, t, g)),
        out_shape=jax.ShapeDtypeStruct((bsz, seq, NSA_W), BF16),
        scratch_shapes=[pltpu.VMEM((tq // NSA_SQ * (seq // SLC_LEN), NSA_SQ), F32)],
        compiler_params=_params(("parallel", "parallel", "arbitrary"), 40),
        name="nsa_attn",
    )(p_nsa, cosf, sins, kcmp, vcmp, ksr, vsb, kwr, vwb, p_sm)


def _rope_tables(seq):
    inv = ROPE_THETA ** (-jnp.arange(0, HEAD_DIM, 2, dtype=F32) / HEAD_DIM)
    ang = jnp.arange(seq, dtype=F32)[:, None] * inv[None, :]
    cos, sin = jnp.cos(ang), jnp.sin(ang)
    return jnp.concatenate([cos, cos], axis=-1), jnp.concatenate([-sin, sin], axis=-1)


def _mixers(p_hg, p_gd, p_nsa, p_sm, lb, hg_norm, conv_w, a_log, dt_bias, gd_norm,
            posk, w1k, w2k, posv, w1v, w2v, cosf, sins):
    o_a = _hgrn(p_hg, lb, hg_norm)
    o_b = _gdn(p_gd, p_sm, conv_w, a_log, dt_bias, gd_norm)
    prep = _nsa_prep(p_nsa, cosf, sins, posk, w1k.astype(BF16), w2k.astype(BF16),
                     posv, w1v.astype(BF16), w2v.astype(BF16))
    o_c = _nsa_attn(p_nsa, p_sm, cosf, sins, *prep)
    return o_a, o_b, o_c


def kernel(x, attn_norm, w_in, w_out, ffn_norm, w_gate, w_up, w_down, final_norm, hgrn_lb_logits, hgrn_out_norm, gdn_conv, gdn_A_log, gdn_dt_bias, gdn_out_norm, cmp_pos_k, cmp_w1_k, cmp_w2_k, cmp_pos_v, cmp_w1_v, cmp_w2_v):
    bsz, seq, d = x.shape
    depth = w_in.shape[0]
    n = bsz * seq
    cosf, sins = _rope_tables(seq)
    lb_p = jax.nn.softmax(hgrn_lb_logits.astype(F32), axis=0)
    lb_all = jnp.concatenate([jnp.zeros_like(lb_p[:1]), jnp.cumsum(lb_p, axis=0)[:-1]], axis=0)

    c_gd = 4 * HG_W
    c_ab = c_gd + 4 * GD_W
    c_nsa = c_ab + 2 * GD_HEADS
    c_gate = c_nsa + NSA_W + 6 * NSA_KV_W
    n_gate = NSA_HEADS * 3

    w_all = w_in.astype(BF16)
    w_nsa = w_all[:, :, c_nsa:c_gate]
    w_sm = jnp.concatenate(
        [w_all[:, :, c_ab:c_nsa], w_all[:, :, c_gate:],
         jnp.zeros((depth, d, SM_W - 2 * GD_HEADS - n_gate), BF16)], axis=2)
    w_dn = w_down.astype(BF16)

    xf = x.reshape(n, d)
    for l in range(depth):
        h = _rmsnorm(xf, attn_norm[l], BF16)
        p_hg = _matmul(h, w_all, l, 0, c_gd).reshape(bsz, seq, -1)
        p_gd = _matmul(h, w_all, l, c_gd, c_ab - c_gd).reshape(bsz, seq, -1)
        p_nsa = _matmul(h, w_nsa, l).reshape(bsz, seq, -1)
        p_sm = _matmul(h, w_sm, l).reshape(bsz, seq, -1)
        mix = _mixers(p_hg, p_gd, p_nsa, p_sm, lb_all[l], hgrn_out_norm[l], gdn_conv[l],
                      gdn_A_log[l], gdn_dt_bias[l], gdn_out_norm[l],
                      cmp_pos_k[l], cmp_w1_k[l], cmp_w2_k[l], cmp_pos_v[l], cmp_w1_v[l], cmp_w2_v[l],
                      cosf, sins)
        xf = _out_proj([o.reshape(n, o.shape[-1]) for o in mix], w_out, l, xf)

        h = _rmsnorm(xf, ffn_norm[l], BF16)
        act = _gate_up(h, w_gate, w_up, l)
        xf = _matmul_ktiled_res(act, w_dn, l, xf)
    return _rmsnorm(xf, final_norm, F32).reshape(bsz, seq, d)
```

```python
import functools

import jax
import jax.numpy as jnp
from jax import lax
from jax.experimental import pallas as pl
from jax.experimental.pallas import tpu as pltpu

F32 = jnp.float32
BF16 = jnp.bfloat16
HI = lax.Precision.HIGHEST

D_MODEL = 4096
HEAD_DIM = 128
HG_HEADS = 8
GD_HEADS = 8
CONV_W = 4
NSA_HEADS = 16
NSA_KV = 4
NSA_HPG = NSA_HEADS // NSA_KV
CMP_LEN = 32
CMP_STRIDE = 16
CMP_HIDDEN = 256
SLC_LEN = 64
SLC_TOP = 16
WINDOW = 512
ROPE_THETA = 10000.0
NORM_EPS = 1e-6
LB_FLOOR = 1e-30
NEG = -1e30
LOG2E = 1.4426950408889634
FFN_HIDDEN = 11008

HG_W = HG_HEADS * HEAD_DIM
GD_W = GD_HEADS * HEAD_DIM
NSA_W = NSA_HEADS * HEAD_DIM
NSA_KV_W = NSA_KV * HEAD_DIM
SM_W = 128
GATE_COL0 = 2 * GD_HEADS

LANES = 128
MIB = 1 << 20

NT = (((1,), (1,)), ((), ()))
TN = (((0,), (0,)), ((), ()))


def _params(sem, vmem_mib):
    return pltpu.CompilerParams(dimension_semantics=sem, vmem_limit_bytes=vmem_mib * MIB)


def _silu(x):
    return x * jax.nn.sigmoid(x)


def _softplus(x):
    return jnp.maximum(x, 0.0) + jnp.log1p(jnp.exp(-jnp.abs(x)))


def _lane_pick(x, lane_ids, idx):
    return jnp.sum(jnp.where(lane_ids == idx, x, 0.0), axis=-1, keepdims=True)


def _rms_kernel(x_ref, w_ref, o_ref):
    x = x_ref[...]
    ms = jnp.mean(x * x, axis=-1, keepdims=True)
    o_ref[...] = (x * lax.rsqrt(ms + NORM_EPS) * w_ref[...]).astype(o_ref.dtype)


def _rmsnorm(x, w, out_dtype, rows=256):
    n, d = x.shape
    return pl.pallas_call(
        _rms_kernel,
        grid=(n // rows,),
        in_specs=[pl.BlockSpec((rows, d), lambda i: (i, 0)),
                  pl.BlockSpec((1, d), lambda i: (0, 0))],
        out_specs=pl.BlockSpec((rows, d), lambda i: (i, 0)),
        out_shape=jax.ShapeDtypeStruct((n, d), out_dtype),
        compiler_params=_params(("parallel",), 40),
        name="rmsnorm",
    )(x, w.reshape(1, d))


def _mm_kernel(a_ref, b_ref, o_ref):
    o_ref[...] = jnp.dot(a_ref[...], b_ref[...].astype(BF16), preferred_element_type=F32).astype(o_ref.dtype)


def _mm_nt_kernel(a_ref, b_ref, o_ref):
    o_ref[...] = lax.dot_general(a_ref[...], b_ref[...].astype(BF16), NT,
                                 preferred_element_type=F32).astype(o_ref.dtype)


def _mm_acc_res_kernel(a_ref, b_ref, r_ref, o_ref, acc_ref, *, nk):
    k = pl.program_id(2)

    @pl.when(k == 0)
    def _():
        acc_ref[...] = r_ref[...]

    acc_ref[...] += jnp.dot(a_ref[...], b_ref[...], preferred_element_type=F32)

    @pl.when(k == nk - 1)
    def _():
        o_ref[...] = acc_ref[...]


def _matmul(a, w, layer=0, col0=0, n=None, transposed=False, tm=1024, tn=512, out_dtype=F32):
    m, k = a.shape
    n = w.shape[1 if transposed else 2] if n is None else n
    tn = min(tn, n)
    cb = col0 // tn
    if transposed:
        w_spec = pl.BlockSpec((None, tn, k), lambda i, j: (layer, cb + j, 0))
    else:
        w_spec = pl.BlockSpec((None, k, tn), lambda i, j: (layer, 0, cb + j))
    return pl.pallas_call(
        _mm_nt_kernel if transposed else _mm_kernel,
        grid=(m // tm, n // tn),
        in_specs=[pl.BlockSpec((tm, k), lambda i, j: (i, 0)), w_spec],
        out_specs=pl.BlockSpec((tm, tn), lambda i, j: (i, j)),
        out_shape=jax.ShapeDtypeStruct((m, n), out_dtype),
        compiler_params=_params(("parallel", "parallel"), 56),
        name="matmul",
    )(a, w)


def _out_proj_kernel(a_ref, b_ref, c_ref, w_ref, r_ref, o_ref):
    ka, kb = a_ref.shape[1], b_ref.shape[1]
    acc = r_ref[...] + jnp.dot(a_ref[...], w_ref[0:ka, :].astype(BF16), preferred_element_type=F32)
    acc = acc + jnp.dot(b_ref[...], w_ref[ka:ka + kb, :].astype(BF16), preferred_element_type=F32)
    o_ref[...] = acc + jnp.dot(c_ref[...], w_ref[ka + kb:, :].astype(BF16), preferred_element_type=F32)


def _out_proj(parts, w, layer, res, tm=1024, tn=512):
    m, n = res.shape
    k = w.shape[1]
    return pl.pallas_call(
        _out_proj_kernel,
        grid=(m // tm, n // tn),
        in_specs=[pl.BlockSpec((tm, p.shape[1]), lambda i, j: (i, 0)) for p in parts]
        + [pl.BlockSpec((None, k, tn), lambda i, j: (layer, 0, j)),
           pl.BlockSpec((tm, tn), lambda i, j: (i, j))],
        out_specs=pl.BlockSpec((tm, tn), lambda i, j: (i, j)),
        out_shape=jax.ShapeDtypeStruct((m, n), F32),
        compiler_params=_params(("parallel", "parallel"), 56),
        name="out_proj",
    )(*parts, w, res)


def _matmul_ktiled_res(a, b, layer, res, tm=1024, tn=512, tk=5504):
    m, k = a.shape
    n = b.shape[2]
    nk = k // tk
    return pl.pallas_call(
        functools.partial(_mm_acc_res_kernel, nk=nk),
        grid=(m // tm, n // tn, nk),
        in_specs=[pl.BlockSpec((tm, tk), lambda i, j, kk: (i, kk)),
                  pl.BlockSpec((None, tk, tn), lambda i, j, kk: (layer, kk, j)),
                  pl.BlockSpec((tm, tn), lambda i, j, kk: (i, j))],
        out_specs=pl.BlockSpec((tm, tn), lambda i, j, kk: (i, j)),
        out_shape=jax.ShapeDtypeStruct((m, n), F32),
        scratch_shapes=[pltpu.VMEM((tm, tn), F32)],
        compiler_params=_params(("parallel", "parallel", "arbitrary"), 48),
        name="matmul_down",
    )(a, b, res)


def _gate_up_kernel(a_ref, wg_ref, wu_ref, o_ref):
    a = a_ref[...]
    g = jnp.dot(a, wg_ref[...].astype(BF16), preferred_element_type=F32)
    u = jnp.dot(a, wu_ref[...].astype(BF16), preferred_element_type=F32)
    o_ref[...] = (_silu(g) * u).astype(o_ref.dtype)


def _gate_up(a, wg, wu, layer, tm=1024, tn=256):
    m, k = a.shape
    n = wg.shape[2]
    return pl.pallas_call(
        _gate_up_kernel,
        grid=(m // tm, n // tn),
        in_specs=[pl.BlockSpec((tm, k), lambda i, j: (i, 0)),
                  pl.BlockSpec((None, k, tn), lambda i, j: (layer, 0, j)),
                  pl.BlockSpec((None, k, tn), lambda i, j: (layer, 0, j))],
        out_specs=pl.BlockSpec((tm, tn), lambda i, j: (i, j)),
        out_shape=jax.ShapeDtypeStruct((m, n), BF16),
        compiler_params=_params(("parallel", "parallel"), 56),
        name="ffn_gate_up",
    )(a, wg, wu)


HG_ROWS = 64
HG_SUB = 16
HG_GROUP = 2


def _hgrn_phases(q_ref, f_ref, i_ref, g_ref, lb_ref, nw_ref, o_ref):
    rows, sub = HG_ROWS, HG_SUB
    row_id = lax.broadcasted_iota(jnp.int32, (rows, LANES), 0)
    sub_id = row_id & (sub - 1)
    r2 = lax.broadcasted_iota(jnp.int32, (rows, rows), 0)
    c2 = lax.broadcasted_iota(jnp.int32, (rows, rows), 1)
    tri = jnp.where(((r2 // sub) == (c2 // sub)) & (c2 <= r2), 1.0, 0.0).astype(BF16)
    log_lb = jnp.log(jnp.maximum(lb_ref[...], LB_FLOOR))
    nw = nw_ref[...]
    nsub = rows // sub

    def features(c):
        r0 = pl.multiple_of(c * rows, rows)
        q = _silu(q_ref[pl.ds(r0, rows), :])
        fp = f_ref[pl.ds(r0, rows), :]
        v = i_ref[pl.ds(r0, rows), :]
        logf = jnp.minimum(-_softplus(-fp) + _softplus(log_lb - fp), 0.0)
        return q, 1.0 - jnp.exp(logf), v, logf

    def phases(i, box):
        cs = [i * HG_GROUP + j for j in range(HG_GROUP)]
        d_ = {}

        def load():
            d_["feats"] = [features(c) for c in cs]

        def cumsum():
            d_["bs"] = [_dot_exact_lhs(tri, f[3]) for f in d_["feats"]]

        def intra():
            feats, bs = d_["feats"], d_["bs"]
            accs = [jnp.sum(q * kf, axis=-1, keepdims=True) * v for q, kf, v, _ in feats]
            for d in range(1, sub):
                prods = [jnp.where(sub_id >= d, q * pltpu.roll(kf, d, 0)
                                   * jnp.exp(jnp.minimum(b - pltpu.roll(b, d, 0), 0.0)), 0.0)
                         for (q, kf, _, _), b in zip(feats, bs)]
                accs = [acc + jnp.sum(p, axis=-1, keepdims=True) * pltpu.roll(f[2], d, 0)
                        for acc, p, f in zip(accs, prods, feats)]
            d_["accs"] = accs

        def products():
            pieces = []
            for (q, kf, v, _), b in zip(d_["feats"], d_["bs"]):
                for j in range(nsub):
                    sl = slice(j * sub, (j + 1) * sub)
                    bj = b[sl]
                    bl = bj[sub - 1:sub]
                    kh = (kf[sl] * jnp.exp(bl - bj)).astype(BF16)
                    vk = lax.dot_general(v[sl].astype(BF16), kh, TN, preferred_element_type=F32)
                    pieces.append(((q[sl] * jnp.exp(bj)).astype(BF16), jnp.exp(bl), vk))
            d_["pieces"] = pieces

        def chain():
            state, inter = box["s"], []
            for qt, decay, vk in d_["pieces"]:
                inter.append(lax.dot_general(qt, state.astype(BF16), NT, preferred_element_type=F32))
                state = decay * state + vk
            box["s"], d_["inter"] = state, inter

        def emit():
            for n, c in enumerate(cs):
                r0 = pl.multiple_of(c * rows, rows)
                o = d_["accs"][n] + jnp.concatenate(d_["inter"][n * nsub:(n + 1) * nsub], axis=0)
                ms = jnp.mean(o * o, axis=-1, keepdims=True)
                y = o * lax.rsqrt(ms + NORM_EPS) * nw
                y = y * _silu(g_ref[pl.ds(r0, rows), :])
                o_ref[pl.ds(r0, rows), :] = y.astype(o_ref.dtype)

        return [load, cumsum, intra, products, chain, emit]

    return phases


def _hgrn_kernel(*refs, seq):
    phases = _hgrn_phases(*refs)

    def step(i, state):
        box = {"s": state}
        for phase in phases(i, box):
            phase()
        return box["s"]

    lax.fori_loop(0, seq // (HG_ROWS * HG_GROUP), step, jnp.zeros((LANES, LANES), F32))


def _hgrn(p_hg, lb, norm_w):
    bsz, seq, _ = p_hg.shape
    assert seq % (HG_ROWS * HG_GROUP) == 0
    nh = HG_HEADS

    def part(k):
        return pl.BlockSpec((None, seq, LANES), lambda b, h, k=k: (b, 0, h + nh * k))

    return pl.pallas_call(
        functools.partial(_hgrn_kernel, seq=seq),
        grid=(bsz, nh),
        in_specs=[part(0), part(1), part(2), part(3),
                  pl.BlockSpec((1, LANES), lambda b, h: (0, h)),
                  pl.BlockSpec((1, LANES), lambda b, h: (0, 0))],
        out_specs=pl.BlockSpec((None, seq, LANES), lambda b, h: (b, 0, h)),
        out_shape=jax.ShapeDtypeStruct((bsz, seq, HG_W), BF16),
        compiler_params=_params(("parallel", "parallel"), 32),
        name="hgrn2",
    )(p_hg, p_hg, p_hg, p_hg, lb.reshape(1, HG_W), norm_w.reshape(1, LANES))


GD_CHUNK = 64
GD_GROUP = 8


def _split2(x):
    hi = x.astype(BF16)
    return hi, (x - hi.astype(F32)).astype(BF16)


def _dot_x3(a, b):
    ah, al = _split2(a)
    bh, bl = _split2(b)
    d = lambda x, y: jnp.dot(x, y, preferred_element_type=F32)
    return d(ah, bh) + (d(ah, bl) + d(al, bh))


def _dot_exact_lhs(a, b):
    hi = b.astype(BF16)
    r = b - hi.astype(F32)
    mid = r.astype(BF16)
    lo = (r - mid.astype(F32)).astype(BF16)
    d = lambda y: jnp.dot(a, y, preferred_element_type=F32)
    return d(hi) + (d(mid) + d(lo))


def _gdn_program(q_ref, k_ref, v_ref, z_ref, sm_ref, cq_ref, ck_ref, cv_ref, al_ref, dtb_ref, nw_ref,
                 o_ref, am_s, bm_s, qp_s, op_s, dl_s):
    ch = GD_CHUNK
    h = pl.program_id(1)
    lane1 = lax.broadcasted_iota(jnp.int32, (1, LANES), 1)
    lane = lax.broadcasted_iota(jnp.int32, (ch, LANES), 1)
    neg_a = -jnp.exp(_lane_pick(al_ref[...], lane1, h))
    dt_bias = _lane_pick(dtb_ref[...], lane1, h)
    r2 = lax.broadcasted_iota(jnp.int32, (ch, ch), 0)
    c2 = lax.broadcasted_iota(jnp.int32, (ch, ch), 1)
    incl = c2 <= r2
    strict = c2 < r2
    tri = jnp.where(incl, 1.0, 0.0).astype(BF16)
    eye = jnp.where(c2 == r2, 1.0, 0.0).astype(F32)
    nw = nw_ref[...]

    def conv_silu(ref, w_ref, r0, c):
        cur = ref[pl.ds(r0, ch), :]
        prev = ref[pl.ds(jnp.maximum(r0 - 8, 0), 8), :]
        prev = jnp.where(c > 0, prev, 0.0)
        x = jnp.concatenate([prev, cur], axis=0)
        w = w_ref[...]
        y = cur * w[CONV_W - 1:CONV_W]
        for s in range(1, CONV_W):
            y = y + pltpu.roll(x, s, 0)[8:] * w[CONV_W - 1 - s:CONV_W - s]
        return _silu(y)

    def features(c):
        r0 = pl.multiple_of(c * ch, ch)
        qf = conv_silu(q_ref, cq_ref, r0, c)
        kf = conv_silu(k_ref, ck_ref, r0, c)
        vf = conv_silu(v_ref, cv_ref, r0, c)
        qf = qf * lax.rsqrt(jnp.sum(qf * qf, axis=-1, keepdims=True) + NORM_EPS) * (HEAD_DIM ** -0.5)
        kf = kf * lax.rsqrt(jnp.sum(kf * kf, axis=-1, keepdims=True) + NORM_EPS)
        sm = sm_ref[pl.ds(r0, ch), :]
        a = _lane_pick(sm, lane, h)
        beta = jax.nn.sigmoid(_lane_pick(sm, lane, h + GD_HEADS))
        g = neg_a * _softplus(a + dt_bias)
        return qf, kf, vf, beta, jnp.broadcast_to(g, (ch, LANES))

    def stages(i):
        cs = [i * GD_GROUP + j for j in range(GD_GROUP)]
        d_ = {}

        def load():
            d_["feats"] = [features(c) for c in cs]

        def cumsum():
            d_["gcs"] = [_dot_exact_lhs(tri, f[4]) for f in d_["feats"]]

        def gram():
            feats = d_["feats"]
            d_["decays"] = [jnp.where(incl, jnp.exp(jnp.minimum(gc[:, :ch] - jnp.transpose(gc)[:ch, :], 0.0)), 0.0)
                            for gc in d_["gcs"]]
            kbs = [f[1].astype(BF16) for f in feats]
            kks = [lax.dot_general(kb, kb, NT, preferred_element_type=F32) for kb in kbs]
            d_["qks"] = [lax.dot_general(f[0].astype(BF16), kb, NT, preferred_element_type=F32)
                         for f, kb in zip(feats, kbs)]
            d_["pws"] = [jnp.where(strict, -(f[3] * kk * d), 0.0) for f, kk, d in zip(feats, kks, d_["decays"])]
            d_["invs"] = [eye + pw for pw in d_["pws"]]

        def square():
            d_["pws"] = [_dot_x3(pw, pw) for pw in d_["pws"]]
            d_["invs"] = [inv + _dot_x3(inv, pw) for inv, pw in zip(d_["invs"], d_["pws"])]

        def solve():
            d_["egcs"] = [jnp.exp(gc) for gc in d_["gcs"]]
            d_["wus"] = [_dot_x3(inv, jnp.concatenate([f[1] * (f[3] * egc), f[2] * f[3]], axis=1)).astype(BF16)
                         for inv, f, egc in zip(d_["invs"], d_["feats"], d_["egcs"])]

        def maps():
            d_["qos"] = [jnp.dot(jnp.where(incl, qk * d, 0.0).astype(BF16), wu, preferred_element_type=F32)
                         for qk, d, wu in zip(d_["qks"], d_["decays"], d_["wus"])]
            d_["abs"] = [lax.dot_general((f[1] * jnp.exp(gc[ch - 1:ch] - gc)).astype(BF16), wu, TN,
                                         preferred_element_type=F32)
                         for f, gc, wu in zip(d_["feats"], d_["gcs"], d_["wus"])]

        def store():
            for c, f, gc, egc, qo, ab in zip(cs, d_["feats"], d_["gcs"], d_["egcs"], d_["qos"], d_["abs"]):
                r0 = pl.multiple_of(c * ch, ch)
                m0 = pl.multiple_of(c * LANES, LANES)
                am_s[pl.ds(m0, LANES), :] = ab[:, :LANES].astype(BF16)
                bm_s[pl.ds(m0, LANES), :] = ab[:, LANES:]
                qp_s[pl.ds(r0, ch), :] = (f[0] * egc - qo[:, :LANES]).astype(BF16)
                op_s[pl.ds(r0, ch), :] = qo[:, LANES:]
                dl_s[pl.ds(pl.multiple_of(c * 8, 8), 8), :] = jnp.broadcast_to(jnp.exp(gc[ch - 1:ch]), (8, LANES))

        return [load, cumsum, gram] + [square] * 5 + [solve, maps, store]

    def recur(c, state):
        r0 = pl.multiple_of(c * ch, ch)
        m0 = pl.multiple_of(c * LANES, LANES)
        sb = state.astype(BF16)
        o = jnp.dot(qp_s[pl.ds(r0, ch), :], sb, preferred_element_type=F32) + op_s[pl.ds(r0, ch), :]
        dl = dl_s[pl.ds(pl.multiple_of(c * 8, 8), 1), :]
        state = (dl * state - jnp.dot(am_s[pl.ds(m0, LANES), :], sb, preferred_element_type=F32)
                 + bm_s[pl.ds(m0, LANES), :])
        ms = jnp.mean(o * o, axis=-1, keepdims=True)
        y = o * lax.rsqrt(ms + NORM_EPS) * nw
        y = y * _silu(z_ref[pl.ds(r0, ch), :])
        o_ref[pl.ds(r0, ch), :] = y.astype(o_ref.dtype)
        return state

    return stages, recur


def _gdn_kernel(*refs, seq):
    stages, recur = _gdn_program(*refs)
    nchunks = seq // GD_CHUNK

    def prepare(i, carry):
        for stage in stages(i):
            stage()
        return carry

    lax.fori_loop(0, nchunks // GD_GROUP, prepare, 0)
    lax.fori_loop(0, nchunks, recur, jnp.zeros((LANES, LANES), F32), unroll=2)


_FUSED_ORDER = ("g0", (0, 0), "g1", (0, 1), "g2", (0, 2), "g3", (0, 3), (0, 4), (0, 5), (1, 0), (1, 1),
                "g4", (1, 2), "g5", (1, 3), (1, 4), (1, 5), (2, 0), (2, 1), "g6", (2, 2), "g7",
                (2, 3), (2, 4), (2, 5), (3, 0), (3, 1), "g8", (3, 2), "g9", (3, 3), (3, 4), (3, 5), "g10")


def _hgrn_gdn_kernel(*refs, seq):
    hg_refs, gd_refs = refs[:6] + refs[17:18], refs[6:17] + refs[18:]
    phases = _hgrn_phases(*hg_refs)
    stages, recur = _gdn_program(*gd_refs)
    nchunks = seq // GD_CHUNK
    steps = GD_GROUP * GD_CHUNK // (HG_ROWS * HG_GROUP)

    ngroups = nchunks // GD_GROUP
    every = len(_FUSED_ORDER) // GD_GROUP

    def body(i, carry, recur_previous):
        box = {"s": carry[0]}
        gstate = carry[1]
        gd = stages(i)
        hg = [phases(i * steps + k, box) for k in range(steps)]
        for n, item in enumerate(_FUSED_ORDER):
            if isinstance(item, str):
                gd[int(item[1:])]()
            else:
                hg[item[0]][item[1]]()
            if recur_previous and n % every == every - 1 and n // every < GD_GROUP:
                gstate = recur((i - 1) * GD_GROUP + n // every, gstate)
        return box["s"], gstate

    zero = jnp.zeros((LANES, LANES), F32)
    carry = body(jnp.int32(0), (zero, zero), False)
    carry = lax.fori_loop(1, ngroups, lambda i, c: body(i, c, True), carry)
    lax.fori_loop(nchunks - GD_GROUP, nchunks, recur, carry[1], unroll=2)


def _gdn(p_gd, p_sm, conv_w, a_log, dt_bias, norm_w):
    bsz, seq, _ = p_gd.shape
    nh = GD_HEADS
    nchunks = seq // GD_CHUNK
    assert seq % (GD_CHUNK * GD_GROUP) == 0

    def part(k):
        return pl.BlockSpec((None, seq, LANES), lambda b, h, k=k: (b, 0, h + nh * k))

    def cpart(k):
        return pl.BlockSpec((CONV_W, LANES), lambda b, h, k=k: (0, h + nh * k))

    pad = lambda t: jnp.pad(t.astype(F32), (0, LANES - nh)).reshape(1, LANES)
    small = pl.BlockSpec((1, LANES), lambda b, h: (0, 0))
    return pl.pallas_call(
        functools.partial(_gdn_kernel, seq=seq),
        grid=(bsz, nh),
        in_specs=[part(0), part(1), part(2), part(3),
                  pl.BlockSpec((None, seq, SM_W), lambda b, h: (b, 0, 0)),
                  cpart(0), cpart(1), cpart(2), small, small, small],
        out_specs=pl.BlockSpec((None, seq, LANES), lambda b, h: (b, 0, h)),
        out_shape=jax.ShapeDtypeStruct((bsz, seq, GD_W), BF16),
        scratch_shapes=[pltpu.VMEM((nchunks * LANES, LANES), BF16), pltpu.VMEM((nchunks * LANES, LANES), F32),
                        pltpu.VMEM((seq, LANES), BF16), pltpu.VMEM((seq, LANES), F32),
                        pltpu.VMEM((nchunks * 8, LANES), F32)],
        compiler_params=_params(("parallel", "parallel"), 40),
        name="gdn",
    )(p_gd, p_gd, p_gd, p_gd, p_sm, conv_w, conv_w, conv_w, pad(a_log), pad(dt_bias),
      norm_w.reshape(1, LANES))


def _hgrn_gdn(p_hg, lb, hg_norm, p_gd, p_sm, conv_w, a_log, dt_bias, gd_norm):
    bsz, seq, _ = p_hg.shape
    nh = HG_HEADS
    nchunks = seq // GD_CHUNK
    assert HG_HEADS == GD_HEADS and seq % (GD_CHUNK * GD_GROUP) == 0
    assert (GD_GROUP * GD_CHUNK) % (HG_ROWS * HG_GROUP) == 0 and len(_FUSED_ORDER) == 11 + 6 * 4

    def part(k):
        return pl.BlockSpec((None, seq, LANES), lambda b, h, k=k: (b, 0, h + nh * k))

    def cpart(k):
        return pl.BlockSpec((CONV_W, LANES), lambda b, h, k=k: (0, h + nh * k))

    pad = lambda t: jnp.pad(t.astype(F32), (0, LANES - nh)).reshape(1, LANES)
    small = pl.BlockSpec((1, LANES), lambda b, h: (0, 0))
    head_out = pl.BlockSpec((None, seq, LANES), lambda b, h: (b, 0, h))
    return pl.pallas_call(
        functools.partial(_hgrn_gdn_kernel, seq=seq),
        grid=(bsz, nh),
        in_specs=[part(0), part(1), part(2), part(3), pl.BlockSpec((1, LANES), lambda b, h: (0, h)), small,
                  part(0), part(1), part(2), part(3),
                  pl.BlockSpec((None, seq, SM_W), lambda b, h: (b, 0, 0)),
                  cpart(0), cpart(1), cpart(2), small, small, small],
        out_specs=[head_out, head_out],
        out_shape=[jax.ShapeDtypeStruct((bsz, seq, HG_W), BF16), jax.ShapeDtypeStruct((bsz, seq, GD_W), BF16)],
        scratch_shapes=[pltpu.VMEM((nchunks * LANES, LANES), BF16), pltpu.VMEM((nchunks * LANES, LANES), F32),
                        pltpu.VMEM((seq, LANES), BF16), pltpu.VMEM((seq, LANES), F32),
                        pltpu.VMEM((nchunks * 8, LANES), F32)],
        compiler_params=_params(("parallel", "parallel"), 48),
        name="hgrn_gdn",
    )(p_hg, p_hg, p_hg, p_hg, lb.reshape(1, HG_W), hg_norm.reshape(1, LANES),
      p_gd, p_gd, p_gd, p_gd, p_sm, conv_w, conv_w, conv_w, pad(a_log), pad(dt_bias),
      gd_norm.reshape(1, LANES))


NSA_TQ = 256
NSA_SQ = 128
NSA_TK = 256
NSA_RC = 64
ROPE_ROWS = 256


def _gelu_tanh(x):
    return 0.5 * x * (1.0 + jnp.tanh(0.7978845608028654 * (x + 0.044715 * x * x * x)))


def _nsa_prep_kernel(kc_ref, vc_ref, ks_ref, vs_ref, kw_ref, vw_ref, cos_ref, sin_ref,
                     posk_ref, w1k_ref, w2k_ref, posv_ref, w1v_ref, w2v_ref,
                     kcmp_ref, vcmp_ref, ksr_ref, vsb_ref, kwr_ref, vwb_ref, kcr_s, *, seq):
    ncmp = seq // CMP_STRIDE
    per = CMP_LEN // CMP_STRIDE

    def rope_rows(i, carry):
        r0 = pl.multiple_of(i * ROPE_ROWS, ROPE_ROWS)
        rs = pl.ds(r0, ROPE_ROWS)
        cosf = cos_ref[rs, :]
        sins = sin_ref[rs, :]

        def rope(x):
            return x * cosf + pltpu.roll(x, HEAD_DIM // 2, 1) * sins

        kcr_s[rs, :] = rope(kc_ref[rs, :])
        ksr_ref[rs, :] = rope(ks_ref[rs, :]).astype(BF16)
        kwr_ref[rs, :] = rope(kw_ref[rs, :]).astype(BF16)
        vsb_ref[:, rs] = jnp.transpose(vs_ref[rs, :]).astype(BF16)
        vwb_ref[:, rs] = jnp.transpose(vw_ref[rs, :]).astype(BF16)
        return carry

    lax.fori_loop(0, seq // ROPE_ROWS, rope_rows, 0)

    def compress(src_ref, pos_ref, w1_ref, w2_ref, out_ref, transpose_out):
        parts = [jnp.zeros((ncmp, CMP_HIDDEN), F32) for _ in range(per)]
        for j in range(CMP_STRIDE):
            xj = src_ref[pl.ds(j, ncmp, stride=CMP_STRIDE), :]
            for p in range(per):
                jj = p * CMP_STRIDE + j
                xp = (xj + pos_ref[jj:jj + 1, :]).astype(BF16)
                parts[p] = parts[p] + jnp.dot(xp, w1_ref[jj * HEAD_DIM:(jj + 1) * HEAD_DIM, :],
                                              preferred_element_type=F32)
        pre = parts[0]
        for p in range(1, per):
            pre = pre + pltpu.roll(parts[p], ncmp - p, 0)
        hid = _gelu_tanh(pre)
        out = jnp.dot(hid.astype(BF16), w2_ref[...], preferred_element_type=F32)
        out_ref[...] = (jnp.transpose(out) if transpose_out else out).astype(BF16)

    compress(kcr_s, posk_ref, w1k_ref, w2k_ref, kcmp_ref, False)
    compress(vc_ref, posv_ref, w1v_ref, w2v_ref, vcmp_ref, True)


def _nsa_prep(p_nsa, cosf, sins, posk, w1k, w2k, posv, w1v, w2v):
    bsz, seq, _ = p_nsa.shape
    ng = NSA_KV
    ncmp = seq // CMP_STRIDE
    q_blocks = NSA_W // LANES

    def part(k):
        return pl.BlockSpec((None, seq, LANES), lambda b, g, k=k: (b, 0, q_blocks + ng * k + g))

    def whole(shape):
        return pl.BlockSpec(shape, lambda b, g: (0,) * len(shape))

    seq_out = pl.BlockSpec((None, None, seq, LANES), lambda b, g: (b, g, 0, 0))
    seq_out_t = pl.BlockSpec((None, None, LANES, seq), lambda b, g: (b, g, 0, 0))
    cmp_out = pl.BlockSpec((None, None, ncmp, LANES), lambda b, g: (b, g, 0, 0))
    seq_shape = jax.ShapeDtypeStruct((bsz, ng, seq, LANES), BF16)
    seq_shape_t = jax.ShapeDtypeStruct((bsz, ng, LANES, seq), BF16)
    cmp_shape = jax.ShapeDtypeStruct((bsz, ng, ncmp, LANES), BF16)
    return pl.pallas_call(
        functools.partial(_nsa_prep_kernel, seq=seq),
        grid=(bsz, ng),
        in_specs=[part(0), part(1), part(2), part(3), part(4), part(5),
                  whole((seq, LANES)), whole((seq, LANES)),
                  whole((CMP_LEN, HEAD_DIM)), whole((CMP_LEN * HEAD_DIM, CMP_HIDDEN)), whole((CMP_HIDDEN, HEAD_DIM)),
                  whole((CMP_LEN, HEAD_DIM)), whole((CMP_LEN * HEAD_DIM, CMP_HIDDEN)), whole((CMP_HIDDEN, HEAD_DIM))],
        out_specs=[cmp_out, cmp_out, seq_out, seq_out_t, seq_out, seq_out_t],
        out_shape=[cmp_shape, cmp_shape, seq_shape, seq_shape_t, seq_shape, seq_shape_t],
        scratch_shapes=[pltpu.VMEM((seq, LANES), F32)],
        compiler_params=_params(("parallel", "parallel"), 48),
        name="nsa_prep",
    )(p_nsa, p_nsa, p_nsa, p_nsa, p_nsa, p_nsa, cosf, sins, posk, w1k, w2k, posv, w1v, w2v)


def _nsa_attn_kernel(q_ref, cos_ref, sin_ref, kcmp_ref, vcmp_ref, ks_ref, vs_ref, kw_ref, vw_ref,
                     sm_ref, o_ref, sel_s, *, seq):
    tq, sq, tk, hpg = NSA_TQ, NSA_SQ, NSA_TK, NSA_HPG
    streams = range(tq // sq)
    cols = hpg * sq
    scale = HEAD_DIM ** -0.5
    g = pl.program_id(1)
    t0 = pl.program_id(2) * tq
    ncmp = seq // CMP_STRIDE
    nslc = seq // SLC_LEN
    kcmp = kcmp_ref[...]
    vcmp_t = vcmp_ref[...]
    heads = [slice(hh * sq, (hh + 1) * sq) for hh in range(hpg)]

    def rope_q_t(st):
        rs = slice(st * sq, (st + 1) * sq)
        cosf = cos_ref[rs, :] * (scale * LOG2E)
        sins = sin_ref[rs, :] * (scale * LOG2E)
        out = []
        for hh in range(hpg):
            x = q_ref[rs, hh * HEAD_DIM:(hh + 1) * HEAD_DIM]
            out.append(jnp.transpose(x * cosf + pltpu.roll(x, HEAD_DIM // 2, 1) * sins).astype(BF16))
        return jnp.concatenate(out, axis=1)

    qts = [rope_q_t(st) for st in streams]
    t_ls = [t0 + st * sq + lax.broadcasted_iota(jnp.int32, (1, sq), 1) for st in streams]

    n_sub = lax.broadcasted_iota(jnp.int32, (ncmp, 1), 0)
    cmasks = [(n_sub * CMP_STRIDE + (CMP_LEN - 1)) <= t_l for t_l in t_ls]
    ss = [jnp.dot(kcmp, qt, preferred_element_type=F32) for qt in qts]
    ss = [[jnp.where(cm, s[:, hd], NEG) for hd in heads] for cm, s in zip(cmasks, ss)]
    es = [[jnp.exp2(s - jnp.max(s, axis=0, keepdims=True)) for s in sh] for sh in ss]
    p_cmps = [[jnp.where(cm, e * (1.0 / jnp.sum(e, axis=0, keepdims=True)), 0.0) for e in eh]
              for cm, eh in zip(cmasks, es)]
    o_cmps = [jnp.dot(vcmp_t, jnp.concatenate(ph, axis=1).astype(BF16), preferred_element_type=F32)
              for ph in p_cmps]

    j_i = lax.broadcasted_iota(jnp.int32, (nslc, ncmp), 0)
    n_i = lax.broadcasted_iota(jnp.int32, (nslc, ncmp), 1)
    ovl = (jnp.minimum(n_i * CMP_STRIDE + CMP_LEN, j_i * SLC_LEN + SLC_LEN)
           - jnp.maximum(n_i * CMP_STRIDE, j_i * SLC_LEN))
    ovl = jnp.maximum(ovl, 0).astype(F32) * (1.0 / CMP_LEN)
    imps = [jnp.dot(ovl, functools.reduce(lambda a, b: a + b, ph), precision=HI, preferred_element_type=F32)
            for ph in p_cmps]
    blk = lax.broadcasted_iota(jnp.int32, (nslc, 1), 0)
    imps = [jnp.where((blk == 0) | (blk == t_l // SLC_LEN) | (blk == t_l // SLC_LEN - 1), jnp.inf, imp)
            for imp, t_l in zip(imps, t_ls)]
    imps = [jnp.where(blk * SLC_LEN <= t_l, imp, -jnp.inf) for imp, t_l in zip(imps, t_ls)]
    ranks = [jnp.zeros((nslc, sq), F32) for _ in streams]
    for i in range(nslc):
        ranks = [rank + jnp.where((imp[i:i + 1, :] > imp) | ((imp[i:i + 1, :] == imp) & (blk > i)), 1.0, 0.0)
                 for rank, imp in zip(ranks, imps)]
    for st, rank in zip(streams, ranks):
        sel_s[st * nslc:(st + 1) * nslc, :] = jnp.where(rank < float(min(SLC_TOP, nslc)), 1.0, 0.0)

    key_sub = lax.broadcasted_iota(jnp.int32, (tk, 1), 0)
    slc_sub = lax.broadcasted_iota(jnp.int32, (SLC_LEN, 1), 0)

    units = [(st, hd) for st in streams for hd in heads]

    def attend(kb, carries, jobs):
        k0 = pl.multiple_of(kb * tk, tk)
        work = []
        for j, (k_ref, vt_ref, bias_fn) in enumerate(jobs):
            kblk = k_ref[pl.ds(k0, tk), :]
            vblk_t = vt_ref[:, pl.ds(k0, tk)]
            biases = bias_fn(k0)
            work += [(j, u, kblk, vblk_t, biases[st], qts[st][:, hd]) for u, (st, hd) in enumerate(units)]

        def scores(w):
            return jnp.dot(work[w][2], work[w][5], preferred_element_type=F32)

        out = [[None] * len(units) for _ in jobs]
        s_next = scores(0)
        for w, (j, u, _, vblk_t, bias, _) in enumerate(work):
            sh = s_next + bias
            if w + 1 < len(work):
                s_next = scores(w + 1)
            m_i, l_i, acc = carries[j][u]
            m_new = jnp.maximum(m_i, jnp.max(sh, axis=0, keepdims=True))
            alpha = jnp.exp2(m_i - m_new)
            p = jnp.exp2(sh - m_new)
            l_new = alpha * l_i + jnp.sum(p, axis=0, keepdims=True)
            pv = jnp.dot(vblk_t, p.astype(BF16), preferred_element_type=F32)
            out[j][u] = (m_new, l_new, alpha * acc + pv)
        return tuple(tuple(o) for o in out)

    init = tuple((jnp.full((1, sq), NEG, F32), jnp.zeros((1, sq), F32), jnp.zeros((HEAD_DIM, sq), F32))
                 for _ in units)

    def slc_bias(k0):
        out = []
        for st, t_l in zip(streams, t_ls):
            pieces = []
            for i in range(tk // SLC_LEN):
                picked = sel_s[pl.ds(st * nslc + k0 // SLC_LEN + i, 1), :]
                kpos = k0 + i * SLC_LEN + slc_sub
                pieces.append(jnp.where((picked > 0.5) & (kpos <= t_l), 0.0, NEG))
            out.append(jnp.concatenate(pieces, axis=0))
        return out

    def win_bias(k0):
        kpos = k0 + key_sub
        return [jnp.where((kpos <= t_l) & (kpos > t_l - WINDOW), 0.0, NEG) for t_l in t_ls]

    n_blocks = (t0 + tq + tk - 1) // tk
    win_lo = jnp.maximum(t0 - WINDOW + 1, 0) // tk
    slc_job = (ks_ref, vs_ref, slc_bias)
    win_job = (kw_ref, vw_ref, win_bias)
    (slc,) = lax.fori_loop(0, win_lo, lambda kb, c: attend(kb, c, [slc_job]), (init,))
    slc, win = lax.fori_loop(win_lo, n_blocks, lambda kb, c: attend(kb, c, [slc_job, win_job]), (slc, init))
    o_slcs = [acc * (1.0 / l) for _, l, acc in slc]
    o_wins = [acc * (1.0 / l) for _, l, acc in win]

    gate_row = lax.broadcasted_iota(jnp.int32, (SM_W, sq), 0)
    for st in streams:
        rs = slice(st * sq, (st + 1) * sq)
        sm_t = jnp.transpose(sm_ref[rs, :])

        def gate(col):
            return jax.nn.sigmoid(jnp.sum(jnp.where(gate_row == col, sm_t, 0.0), axis=0, keepdims=True))

        for hh, hd in enumerate(heads):
            col = GATE_COL0 + (g * hpg + hh) * 3
            u = st * hpg + hh
            o_t = (gate(col) * o_cmps[st][:, hd] + gate(col + 1) * o_slcs[u]
                   + gate(col + 2) * o_wins[u])
            o_ref[rs, hh * HEAD_DIM:(hh + 1) * HEAD_DIM] = jnp.transpose(o_t).astype(o_ref.dtype)


def _nsa_attn(p_nsa, p_sm, cosf, sins, kcmp, vcmp, ksr, vsb, kwr, vwb):
    bsz, seq, _ = p_nsa.shape
    ng = NSA_KV
    tq = NSA_TQ
    assert seq % tq == 0 and seq // CMP_STRIDE == LANES
    ncmp = seq // CMP_STRIDE
    gw = NSA_HPG * HEAD_DIM
    per_group = lambda rows: pl.BlockSpec((None, None, rows, LANES), lambda b, g, t: (b, g, 0, 0))
    per_group_t = pl.BlockSpec((None, None, LANES, seq), lambda b, g, t: (b, g, 0, 0))
    return pl.pallas_call(
        functools.partial(_nsa_attn_kernel, seq=seq),
        grid=(bsz, ng, seq // tq),
        in_specs=[pl.BlockSpec((None, tq, gw), lambda b, g, t: (b, t, g)),
                  pl.BlockSpec((tq, LANES), lambda b, g, t: (t, 0)),
                  pl.BlockSpec((tq, LANES), lambda b, g, t: (t, 0)),
                  per_group(ncmp), per_group(ncmp),
                  per_group(seq), per_group_t, per_group(seq), per_group_t,
                  pl.BlockSpec((None, tq, SM_W), lambda b, g, t: (b, t, 0))],
        out_specs=pl.BlockSpec((None, tq, gw), lambda b, g, t: (b, t, g)),
        out_shape=jax.ShapeDtypeStruct((bsz, seq, NSA_W), BF16),
        scratch_shapes=[pltpu.VMEM((tq // NSA_SQ * (seq // SLC_LEN), NSA_SQ), F32)],
        compiler_params=_params(("parallel", "parallel", "arbitrary"), 40),
        name="nsa_attn",
    )(p_nsa, cosf, sins, kcmp, vcmp, ksr, vsb, kwr, vwb, p_sm)


def _rope_tables(seq):
    inv = ROPE_THETA ** (-jnp.arange(0, HEAD_DIM, 2, dtype=F32) / HEAD_DIM)
    ang = jnp.arange(seq, dtype=F32)[:, None] * inv[None, :]
    cos, sin = jnp.cos(ang), jnp.sin(ang)
    return jnp.concatenate([cos, cos], axis=-1), jnp.concatenate([-sin, sin], axis=-1)


def _mixers(p_hg, p_gd, p_nsa, p_sm, lb, hg_norm, conv_w, a_log, dt_bias, gd_norm,
            posk, w1k, w2k, posv, w1v, w2v, cosf, sins):
    o_a, o_b = _hgrn_gdn(p_hg, lb, hg_norm, p_gd, p_sm, conv_w, a_log, dt_bias, gd_norm)
    prep = _nsa_prep(p_nsa, cosf, sins, posk, w1k.astype(BF16), w2k.astype(BF16),
                     posv, w1v.astype(BF16), w2v.astype(BF16))
    o_c = _nsa_attn(p_nsa, p_sm, cosf, sins, *prep)
    return o_a, o_b, o_c


def kernel(x, attn_norm, w_in, w_out, ffn_norm, w_gate, w_up, w_down, final_norm, hgrn_lb_logits, hgrn_out_norm, gdn_conv, gdn_A_log, gdn_dt_bias, gdn_out_norm, cmp_pos_k, cmp_w1_k, cmp_w2_k, cmp_pos_v, cmp_w1_v, cmp_w2_v):
    bsz, seq, d = x.shape
    depth = w_in.shape[0]
    n = bsz * seq
    cosf, sins = _rope_tables(seq)
    lb_p = jax.nn.softmax(hgrn_lb_logits.astype(F32), axis=0)
    lb_all = jnp.concatenate([jnp.zeros_like(lb_p[:1]), jnp.cumsum(lb_p, axis=0)[:-1]], axis=0)

    c_gd = 4 * HG_W
    c_ab = c_gd + 4 * GD_W
    c_nsa = c_ab + 2 * GD_HEADS
    c_gate = c_nsa + NSA_W + 6 * NSA_KV_W
    n_gate = NSA_HEADS * 3

    w_t = jnp.swapaxes(w_in, 1, 2)
    w_nsa = w_t[:, c_nsa:c_gate, :]
    w_sm = jnp.concatenate(
        [w_t[:, c_ab:c_nsa, :], w_t[:, c_gate:, :],
         jnp.zeros((depth, SM_W - 2 * GD_HEADS - n_gate, d), F32)], axis=1)
    w_dn = w_down.astype(BF16)

    xf = x.reshape(n, d)
    for l in range(depth):
        h = _rmsnorm(xf, attn_norm[l], BF16)
        p_hg = _matmul(h, w_t, l, 0, c_gd, transposed=True).reshape(bsz, seq, -1)
        p_gd = _matmul(h, w_t, l, c_gd, c_ab - c_gd, transposed=True).reshape(bsz, seq, -1)
        p_nsa = _matmul(h, w_nsa, l, transposed=True).reshape(bsz, seq, -1)
        p_sm = _matmul(h, w_sm, l, transposed=True).reshape(bsz, seq, -1)
        mix = _mixers(p_hg, p_gd, p_nsa, p_sm, lb_all[l], hgrn_out_norm[l], gdn_conv[l],
                      gdn_A_log[l], gdn_dt_bias[l], gdn_out_norm[l],
                      cmp_pos_k[l], cmp_w1_k[l], cmp_w2_k[l], cmp_pos_v[l], cmp_w1_v[l], cmp_w2_v[l],
                      cosf, sins)
        xf = _out_proj([o.reshape(n, o.shape[-1]) for o in mix], w_out, l, xf)

        h = _rmsnorm(xf, ffn_norm[l], BF16)
        act = _gate_up(h, w_gate, w_up, l)
        xf = _matmul_ktiled_res(act, w_dn, l, xf)
    return _rmsnorm(xf, final_norm, F32).reshape(bsz, seq, d)
```

```python
import functools

import jax
import jax.numpy as jnp
from jax import lax
from jax.experimental import pallas as pl
from jax.experimental.pallas import tpu as pltpu

F32 = jnp.float32
BF16 = jnp.bfloat16
HI = lax.Precision.HIGHEST

D_MODEL = 4096
HEAD_DIM = 128
HG_HEADS = 8
GD_HEADS = 8
CONV_W = 4
NSA_HEADS = 16
NSA_KV = 4
NSA_HPG = NSA_HEADS // NSA_KV
CMP_LEN = 32
CMP_STRIDE = 16
CMP_HIDDEN = 256
SLC_LEN = 64
SLC_TOP = 16
WINDOW = 512
ROPE_THETA = 10000.0
NORM_EPS = 1e-6
LB_FLOOR = 1e-30
NEG = -1e30
LOG2E = 1.4426950408889634
FFN_HIDDEN = 11008

HG_W = HG_HEADS * HEAD_DIM
GD_W = GD_HEADS * HEAD_DIM
NSA_W = NSA_HEADS * HEAD_DIM
NSA_KV_W = NSA_KV * HEAD_DIM
SM_W = 128
GATE_COL0 = 2 * GD_HEADS

LANES = 128
MIB = 1 << 20

NT = (((1,), (1,)), ((), ()))
TN = (((0,), (0,)), ((), ()))


def _params(sem, vmem_mib):
    return pltpu.CompilerParams(dimension_semantics=sem, vmem_limit_bytes=vmem_mib * MIB)


def _silu(x):
    return x * jax.nn.sigmoid(x)


def _softplus(x):
    return jnp.maximum(x, 0.0) + jnp.log1p(jnp.exp(-jnp.abs(x)))


def _lane_pick(x, lane_ids, idx):
    return jnp.sum(jnp.where(lane_ids == idx, x, 0.0), axis=-1, keepdims=True)


def _rms_kernel(x_ref, w_ref, o_ref):
    x = x_ref[...]
    ms = jnp.mean(x * x, axis=-1, keepdims=True)
    o_ref[...] = (x * lax.rsqrt(ms + NORM_EPS) * w_ref[...]).astype(o_ref.dtype)


def _rmsnorm(x, w, out_dtype, rows=256):
    n, d = x.shape
    return pl.pallas_call(
        _rms_kernel,
        grid=(n // rows,),
        in_specs=[pl.BlockSpec((rows, d), lambda i: (i, 0)),
                  pl.BlockSpec((1, d), lambda i: (0, 0))],
        out_specs=pl.BlockSpec((rows, d), lambda i: (i, 0)),
        out_shape=jax.ShapeDtypeStruct((n, d), out_dtype),
        compiler_params=_params(("parallel",), 40),
        name="rmsnorm",
    )(x, w.reshape(1, d))


def _rinv(ss_ref, d):
    return lax.rsqrt(ss_ref[...] * (1.0 / d) + NORM_EPS)


def _emit_norm_inputs(x, nw_ref, xw_ref, ss_ref, first):
    xw_ref[...] = (x * nw_ref[...]).astype(xw_ref.dtype)
    part = jnp.sum(x * x, axis=-1, keepdims=True)

    @pl.when(first)
    def _():
        ss_ref[...] = part

    @pl.when(jnp.logical_not(first))
    def _():
        ss_ref[...] += part


def _norm_inputs_kernel(x_ref, w_ref, xw_ref, ss_ref):
    x = x_ref[...]
    xw_ref[...] = (x * w_ref[...]).astype(xw_ref.dtype)
    ss_ref[...] = jnp.sum(x * x, axis=-1, keepdims=True)


def _norm_inputs(x, w, rows=256):
    n, d = x.shape
    return pl.pallas_call(
        _norm_inputs_kernel,
        grid=(n // rows,),
        in_specs=[pl.BlockSpec((rows, d), lambda i: (i, 0)),
                  pl.BlockSpec((1, d), lambda i: (0, 0))],
        out_specs=[pl.BlockSpec((rows, d), lambda i: (i, 0)), pl.BlockSpec((rows, 1), lambda i: (i, 0))],
        out_shape=[jax.ShapeDtypeStruct((n, d), BF16), jax.ShapeDtypeStruct((n, 1), F32)],
        compiler_params=_params(("parallel",), 40),
        name="norm_inputs",
    )(x, w.reshape(1, d))


def _mm_kernel(a_ref, b_ref, ss_ref, o_ref):
    acc = jnp.dot(a_ref[...], b_ref[...].astype(BF16), preferred_element_type=F32)
    o_ref[...] = (_rinv(ss_ref, a_ref.shape[1]) * acc).astype(o_ref.dtype)


def _mm_nt_kernel(a_ref, b_ref, ss_ref, o_ref):
    acc = lax.dot_general(a_ref[...], b_ref[...].astype(BF16), NT, preferred_element_type=F32)
    o_ref[...] = (_rinv(ss_ref, a_ref.shape[1]) * acc).astype(o_ref.dtype)


def _matmul(xw, ss, w, layer=0, col0=0, n=None, transposed=False, tm=1024, tn=512, out_dtype=F32):
    m, k = xw.shape
    n = w.shape[1 if transposed else 2] if n is None else n
    tn = min(tn, n)
    cb = col0 // tn
    if transposed:
        w_spec = pl.BlockSpec((None, tn, k), lambda i, j: (layer, cb + j, 0))
    else:
        w_spec = pl.BlockSpec((None, k, tn), lambda i, j: (layer, 0, cb + j))
    return pl.pallas_call(
        _mm_nt_kernel if transposed else _mm_kernel,
        grid=(m // tm, n // tn),
        in_specs=[pl.BlockSpec((tm, k), lambda i, j: (i, 0)), w_spec,
                  pl.BlockSpec((tm, 1), lambda i, j: (i, 0))],
        out_specs=pl.BlockSpec((tm, tn), lambda i, j: (i, j)),
        out_shape=jax.ShapeDtypeStruct((m, n), out_dtype),
        compiler_params=_params(("parallel", "parallel"), 56),
        name="matmul",
    )(xw, w, ss)


def _residual_outputs(m, n, tm, tn, index):
    specs = [pl.BlockSpec((tm, tn), index), pl.BlockSpec((tm, tn), index),
             pl.BlockSpec((tm, 1), lambda i, *_: (i, 0))]
    shapes = [jax.ShapeDtypeStruct((m, n), F32), jax.ShapeDtypeStruct((m, n), BF16),
              jax.ShapeDtypeStruct((m, 1), F32)]
    return specs, shapes


def _out_proj_kernel(a_ref, b_ref, c_ref, w_ref, r_ref, nw_ref, o_ref, xw_ref, ss_ref):
    ka, kb = a_ref.shape[1], b_ref.shape[1]
    acc = r_ref[...] + jnp.dot(a_ref[...], w_ref[0:ka, :].astype(BF16), preferred_element_type=F32)
    acc = acc + jnp.dot(b_ref[...], w_ref[ka:ka + kb, :].astype(BF16), preferred_element_type=F32)
    x = acc + jnp.dot(c_ref[...], w_ref[ka + kb:, :].astype(BF16), preferred_element_type=F32)
    o_ref[...] = x
    _emit_norm_inputs(x, nw_ref, xw_ref, ss_ref, pl.program_id(1) == 0)


def _out_proj(parts, w, layer, res, next_norm_w, tm=1024, tn=512):
    m, n = res.shape
    k = w.shape[1]
    tile = lambda i, j: (i, j)
    out_specs, out_shape = _residual_outputs(m, n, tm, tn, tile)
    return pl.pallas_call(
        _out_proj_kernel,
        grid=(m // tm, n // tn),
        in_specs=[pl.BlockSpec((tm, p.shape[1]), lambda i, j: (i, 0)) for p in parts]
        + [pl.BlockSpec((None, k, tn), lambda i, j: (layer, 0, j)),
           pl.BlockSpec((tm, tn), tile), pl.BlockSpec((1, tn), lambda i, j: (0, j))],
        out_specs=out_specs,
        out_shape=out_shape,
        compiler_params=_params(("parallel", "arbitrary"), 56),
        name="out_proj",
    )(*parts, w, res, next_norm_w.reshape(1, n))


def _mm_acc_res_kernel(a_ref, b_ref, r_ref, nw_ref, o_ref, xw_ref, ss_ref, acc_ref, *, nk):
    k = pl.program_id(2)

    @pl.when(k == 0)
    def _():
        acc_ref[...] = r_ref[...]

    acc_ref[...] += jnp.dot(a_ref[...], b_ref[...], preferred_element_type=F32)

    @pl.when(k == nk - 1)
    def _():
        x = acc_ref[...]
        o_ref[...] = x
        _emit_norm_inputs(x, nw_ref, xw_ref, ss_ref, pl.program_id(1) == 0)


def _matmul_ktiled_res(a, b, layer, res, next_norm_w, tm=1024, tn=512, tk=5504):
    m, k = a.shape
    n = b.shape[2]
    nk = k // tk
    tile = lambda i, j, kk: (i, j)
    out_specs, out_shape = _residual_outputs(m, n, tm, tn, tile)
    return pl.pallas_call(
        functools.partial(_mm_acc_res_kernel, nk=nk),
        grid=(m // tm, n // tn, nk),
        in_specs=[pl.BlockSpec((tm, tk), lambda i, j, kk: (i, kk)),
                  pl.BlockSpec((None, tk, tn), lambda i, j, kk: (layer, kk, j)),
                  pl.BlockSpec((tm, tn), tile), pl.BlockSpec((1, tn), lambda i, j, kk: (0, j))],
        out_specs=out_specs,
        out_shape=out_shape,
        scratch_shapes=[pltpu.VMEM((tm, tn), F32)],
        compiler_params=_params(("parallel", "arbitrary", "arbitrary"), 48),
        name="matmul_down",
    )(a, b, res, next_norm_w.reshape(1, n))


def _gate_up_kernel(a_ref, wg_ref, wu_ref, ss_ref, o_ref):
    a = a_ref[...]
    rinv = _rinv(ss_ref, a_ref.shape[1])
    g = rinv * jnp.dot(a, wg_ref[...].astype(BF16), preferred_element_type=F32)
    u = rinv * jnp.dot(a, wu_ref[...].astype(BF16), preferred_element_type=F32)
    o_ref[...] = (_silu(g) * u).astype(o_ref.dtype)


def _gate_up(xw, ss, wg, wu, layer, tm=1024, tn=256):
    m, k = xw.shape
    n = wg.shape[2]
    return pl.pallas_call(
        _gate_up_kernel,
        grid=(m // tm, n // tn),
        in_specs=[pl.BlockSpec((tm, k), lambda i, j: (i, 0)),
                  pl.BlockSpec((None, k, tn), lambda i, j: (layer, 0, j)),
                  pl.BlockSpec((None, k, tn), lambda i, j: (layer, 0, j)),
                  pl.BlockSpec((tm, 1), lambda i, j: (i, 0))],
        out_specs=pl.BlockSpec((tm, tn), lambda i, j: (i, j)),
        out_shape=jax.ShapeDtypeStruct((m, n), BF16),
        compiler_params=_params(("parallel", "parallel"), 56),
        name="ffn_gate_up",
    )(xw, wg, wu, ss)


HG_ROWS = 64
HG_SUB = 16
HG_GROUP = 2


def _hgrn_phases(q_ref, f_ref, i_ref, g_ref, lb_ref, nw_ref, o_ref):
    rows, sub = HG_ROWS, HG_SUB
    row_id = lax.broadcasted_iota(jnp.int32, (rows, LANES), 0)
    sub_id = row_id & (sub - 1)
    r2 = lax.broadcasted_iota(jnp.int32, (rows, rows), 0)
    c2 = lax.broadcasted_iota(jnp.int32, (rows, rows), 1)
    tri = jnp.where(((r2 // sub) == (c2 // sub)) & (c2 <= r2), 1.0, 0.0).astype(BF16)
    log_lb = jnp.log(jnp.maximum(lb_ref[...], LB_FLOOR))
    nw = nw_ref[...]
    nsub = rows // sub

    def features(c):
        r0 = pl.multiple_of(c * rows, rows)
        q = _silu(q_ref[pl.ds(r0, rows), :])
        fp = f_ref[pl.ds(r0, rows), :]
        v = i_ref[pl.ds(r0, rows), :]
        logf = jnp.minimum(-_softplus(-fp) + _softplus(log_lb - fp), 0.0)
        return q, 1.0 - jnp.exp(logf), v, logf

    def phases(i, box):
        cs = [i * HG_GROUP + j for j in range(HG_GROUP)]
        d_ = {}

        def load():
            d_["feats"] = [features(c) for c in cs]

        def cumsum():
            d_["bs"] = [_dot_exact_lhs(tri, f[3]) * LOG2E for f in d_["feats"]]

        def intra():
            feats, bs = d_["feats"], d_["bs"]
            accs = [jnp.sum(q * kf, axis=-1, keepdims=True) * v for q, kf, v, _ in feats]
            for d in range(1, sub):
                prods = [jnp.where(sub_id >= d, q * pltpu.roll(kf, d, 0)
                                   * jnp.exp2(jnp.minimum(b - pltpu.roll(b, d, 0), 0.0)), 0.0)
                         for (q, kf, _, _), b in zip(feats, bs)]
                accs = [acc + jnp.sum(p, axis=-1, keepdims=True) * pltpu.roll(f[2], d, 0)
                        for acc, p, f in zip(accs, prods, feats)]
            d_["accs"] = accs

        def products():
            pieces = []
            for (q, kf, v, _), b in zip(d_["feats"], d_["bs"]):
                for j in range(nsub):
                    sl = slice(j * sub, (j + 1) * sub)
                    bj = b[sl]
                    bl = bj[sub - 1:sub]
                    kh = (kf[sl] * jnp.exp2(bl - bj)).astype(BF16)
                    vk = lax.dot_general(v[sl].astype(BF16), kh, TN, preferred_element_type=F32)
                    pieces.append(((q[sl] * jnp.exp2(bj)).astype(BF16), jnp.exp2(bl), vk))
            d_["pieces"] = pieces

        def chain():
            state, inter = box["s"], []
            for qt, decay, vk in d_["pieces"]:
                inter.append(lax.dot_general(qt, state.astype(BF16), NT, preferred_element_type=F32))
                state = decay * state + vk
            box["s"], d_["inter"] = state, inter

        def emit():
            for n, c in enumerate(cs):
                r0 = pl.multiple_of(c * rows, rows)
                o = d_["accs"][n] + jnp.concatenate(d_["inter"][n * nsub:(n + 1) * nsub], axis=0)
                ms = jnp.mean(o * o, axis=-1, keepdims=True)
                y = o * lax.rsqrt(ms + NORM_EPS) * nw
                y = y * _silu(g_ref[pl.ds(r0, rows), :])
                o_ref[pl.ds(r0, rows), :] = y.astype(o_ref.dtype)

        return [load, cumsum, intra, products, chain, emit]

    return phases


def _hgrn_kernel(*refs, seq):
    phases = _hgrn_phases(*refs)

    def step(i, state):
        box = {"s": state}
        for phase in phases(i, box):
            phase()
        return box["s"]

    lax.fori_loop(0, seq // (HG_ROWS * HG_GROUP), step, jnp.zeros((LANES, LANES), F32))


def _hgrn(p_hg, lb, norm_w):
    bsz, seq, _ = p_hg.shape
    assert seq % (HG_ROWS * HG_GROUP) == 0
    nh = HG_HEADS

    def part(k):
        return pl.BlockSpec((None, seq, LANES), lambda b, h, k=k: (b, 0, h + nh * k))

    return pl.pallas_call(
        functools.partial(_hgrn_kernel, seq=seq),
        grid=(bsz, nh),
        in_specs=[part(0), part(1), part(2), part(3),
                  pl.BlockSpec((1, LANES), lambda b, h: (0, h)),
                  pl.BlockSpec((1, LANES), lambda b, h: (0, 0))],
        out_specs=pl.BlockSpec((None, seq, LANES), lambda b, h: (b, 0, h)),
        out_shape=jax.ShapeDtypeStruct((bsz, seq, HG_W), BF16),
        compiler_params=_params(("parallel", "parallel"), 32),
        name="hgrn2",
    )(p_hg, p_hg, p_hg, p_hg, lb.reshape(1, HG_W), norm_w.reshape(1, LANES))


GD_CHUNK = 64
GD_GROUP = 8


def _split2(x):
    hi = x.astype(BF16)
    return hi, (x - hi.astype(F32)).astype(BF16)


def _dot_x3(a, b):
    ah, al = _split2(a)
    bh, bl = _split2(b)
    d = lambda x, y: jnp.dot(x, y, preferred_element_type=F32)
    return d(ah, bh) + (d(ah, bl) + d(al, bh))


def _dot_exact_lhs(a, b):
    hi = b.astype(BF16)
    r = b - hi.astype(F32)
    mid = r.astype(BF16)
    lo = (r - mid.astype(F32)).astype(BF16)
    d = lambda y: jnp.dot(a, y, preferred_element_type=F32)
    return d(hi) + (d(mid) + d(lo))


def _gdn_program(q_ref, k_ref, v_ref, z_ref, sm_ref, cq_ref, ck_ref, cv_ref, al_ref, dtb_ref, nw_ref,
                 o_ref, am_s, bm_s, qp_s, op_s, dl_s):
    ch = GD_CHUNK
    h = pl.program_id(1)
    lane1 = lax.broadcasted_iota(jnp.int32, (1, LANES), 1)
    lane = lax.broadcasted_iota(jnp.int32, (ch, LANES), 1)
    neg_a = -jnp.exp(_lane_pick(al_ref[...], lane1, h))
    dt_bias = _lane_pick(dtb_ref[...], lane1, h)
    r2 = lax.broadcasted_iota(jnp.int32, (ch, ch), 0)
    c2 = lax.broadcasted_iota(jnp.int32, (ch, ch), 1)
    incl = c2 <= r2
    strict = c2 < r2
    tri = jnp.where(incl, 1.0, 0.0).astype(BF16)
    eye = jnp.where(c2 == r2, 1.0, 0.0).astype(F32)
    nw = nw_ref[...]

    def conv_silu(ref, w_ref, r0, c):
        cur = ref[pl.ds(r0, ch), :]
        prev = ref[pl.ds(jnp.maximum(r0 - 8, 0), 8), :]
        prev = jnp.where(c > 0, prev, 0.0)
        x = jnp.concatenate([prev, cur], axis=0)
        w = w_ref[...]
        y = cur * w[CONV_W - 1:CONV_W]
        for s in range(1, CONV_W):
            y = y + pltpu.roll(x, s, 0)[8:] * w[CONV_W - 1 - s:CONV_W - s]
        return _silu(y)

    def features(c):
        r0 = pl.multiple_of(c * ch, ch)
        qf = conv_silu(q_ref, cq_ref, r0, c)
        kf = conv_silu(k_ref, ck_ref, r0, c)
        vf = conv_silu(v_ref, cv_ref, r0, c)
        qf = qf * lax.rsqrt(jnp.sum(qf * qf, axis=-1, keepdims=True) + NORM_EPS) * (HEAD_DIM ** -0.5)
        kf = kf * lax.rsqrt(jnp.sum(kf * kf, axis=-1, keepdims=True) + NORM_EPS)
        sm = sm_ref[pl.ds(r0, ch), :]
        a = _lane_pick(sm, lane, h)
        beta = jax.nn.sigmoid(_lane_pick(sm, lane, h + GD_HEADS))
        g = neg_a * _softplus(a + dt_bias)
        return qf, kf, vf, beta, jnp.broadcast_to(g, (ch, LANES))

    def stages(i):
        cs = [i * GD_GROUP + j for j in range(GD_GROUP)]
        d_ = {}

        def load():
            d_["feats"] = [features(c) for c in cs]

        def cumsum():
            d_["gcs"] = [_dot_exact_lhs(tri, f[4]) for f in d_["feats"]]

        def gram():
            feats = d_["feats"]
            d_["decays"] = [jnp.where(incl, jnp.exp(jnp.minimum(gc[:, :ch] - jnp.transpose(gc)[:ch, :], 0.0)), 0.0)
                            for gc in d_["gcs"]]
            kbs = [f[1].astype(BF16) for f in feats]
            kks = [lax.dot_general(kb, kb, NT, preferred_element_type=F32) for kb in kbs]
            d_["qks"] = [lax.dot_general(f[0].astype(BF16), kb, NT, preferred_element_type=F32)
                         for f, kb in zip(feats, kbs)]
            d_["pws"] = [jnp.where(strict, -(f[3] * kk * d), 0.0) for f, kk, d in zip(feats, kks, d_["decays"])]
            d_["invs"] = [eye + pw for pw in d_["pws"]]

        def square():
            d_["pws"] = [_dot_x3(pw, pw) for pw in d_["pws"]]
            d_["invs"] = [inv + _dot_x3(inv, pw) for inv, pw in zip(d_["invs"], d_["pws"])]

        def solve():
            d_["egcs"] = [jnp.exp(gc) for gc in d_["gcs"]]
            d_["wus"] = [_dot_x3(inv, jnp.concatenate([f[1] * (f[3] * egc), f[2] * f[3]], axis=1)).astype(BF16)
                         for inv, f, egc in zip(d_["invs"], d_["feats"], d_["egcs"])]

        def maps():
            d_["qos"] = [jnp.dot(jnp.where(incl, qk * d, 0.0).astype(BF16), wu, preferred_element_type=F32)
                         for qk, d, wu in zip(d_["qks"], d_["decays"], d_["wus"])]
            d_["abs"] = [lax.dot_general((f[1] * jnp.exp(gc[ch - 1:ch] - gc)).astype(BF16), wu, TN,
                                         preferred_element_type=F32)
                         for f, gc, wu in zip(d_["feats"], d_["gcs"], d_["wus"])]

        def store():
            for c, f, gc, egc, qo, ab in zip(cs, d_["feats"], d_["gcs"], d_["egcs"], d_["qos"], d_["abs"]):
                r0 = pl.multiple_of(c * ch, ch)
                m0 = pl.multiple_of(c * LANES, LANES)
                am_s[pl.ds(m0, LANES), :] = ab[:, :LANES].astype(BF16)
                bm_s[pl.ds(m0, LANES), :] = ab[:, LANES:]
                qp_s[pl.ds(r0, ch), :] = (f[0] * egc - qo[:, :LANES]).astype(BF16)
                op_s[pl.ds(r0, ch), :] = qo[:, LANES:]
                dl_s[pl.ds(pl.multiple_of(c * 8, 8), 8), :] = jnp.broadcast_to(jnp.exp(gc[ch - 1:ch]), (8, LANES))

        return [load, cumsum, gram] + [square] * 5 + [solve, maps, store]

    def recur(c, state):
        r0 = pl.multiple_of(c * ch, ch)
        m0 = pl.multiple_of(c * LANES, LANES)
        sb = state.astype(BF16)
        o = jnp.dot(qp_s[pl.ds(r0, ch), :], sb, preferred_element_type=F32) + op_s[pl.ds(r0, ch), :]
        dl = dl_s[pl.ds(pl.multiple_of(c * 8, 8), 1), :]
        state = (dl * state - jnp.dot(am_s[pl.ds(m0, LANES), :], sb, preferred_element_type=F32)
                 + bm_s[pl.ds(m0, LANES), :])
        ms = jnp.mean(o * o, axis=-1, keepdims=True)
        y = o * lax.rsqrt(ms + NORM_EPS) * nw
        y = y * _silu(z_ref[pl.ds(r0, ch), :])
        o_ref[pl.ds(r0, ch), :] = y.astype(o_ref.dtype)
        return state

    return stages, recur


def _gdn_kernel(*refs, seq):
    stages, recur = _gdn_program(*refs)
    nchunks = seq // GD_CHUNK

    def prepare(i, carry):
        for stage in stages(i):
            stage()
        return carry

    lax.fori_loop(0, nchunks // GD_GROUP, prepare, 0)
    lax.fori_loop(0, nchunks, recur, jnp.zeros((LANES, LANES), F32), unroll=2)


_FUSED_ORDER = ("g0", (0, 0), "g1", (0, 1), "g2", (0, 2), "g3", (0, 3), (0, 4), (0, 5), (1, 0), (1, 1),
                "g4", (1, 2), "g5", (1, 3), (1, 4), (1, 5), (2, 0), (2, 1), "g6", (2, 2), "g7",
                (2, 3), (2, 4), (2, 5), (3, 0), (3, 1), "g8", (3, 2), "g9", (3, 3), (3, 4), (3, 5), "g10")


def _hgrn_gdn_kernel(*refs, seq):
    hg_refs, gd_refs = refs[:6] + refs[17:18], refs[6:17] + refs[18:]
    phases = _hgrn_phases(*hg_refs)
    stages, recur = _gdn_program(*gd_refs)
    nchunks = seq // GD_CHUNK
    steps = GD_GROUP * GD_CHUNK // (HG_ROWS * HG_GROUP)

    ngroups = nchunks // GD_GROUP
    every = len(_FUSED_ORDER) // GD_GROUP

    def body(i, carry, recur_previous):
        box = {"s": carry[0]}
        gstate = carry[1]
        gd = stages(i)
        hg = [phases(i * steps + k, box) for k in range(steps)]
        for n, item in enumerate(_FUSED_ORDER):
            if isinstance(item, str):
                gd[int(item[1:])]()
            else:
                hg[item[0]][item[1]]()
            if recur_previous and n % every == every - 1 and n // every < GD_GROUP:
                gstate = recur((i - 1) * GD_GROUP + n // every, gstate)
        return box["s"], gstate

    zero = jnp.zeros((LANES, LANES), F32)
    carry = body(jnp.int32(0), (zero, zero), False)
    carry = lax.fori_loop(1, ngroups, lambda i, c: body(i, c, True), carry)
    lax.fori_loop(nchunks - GD_GROUP, nchunks, recur, carry[1], unroll=2)


def _gdn(p_gd, p_sm, conv_w, a_log, dt_bias, norm_w):
    bsz, seq, _ = p_gd.shape
    nh = GD_HEADS
    nchunks = seq // GD_CHUNK
    assert seq % (GD_CHUNK * GD_GROUP) == 0

    def part(k):
        return pl.BlockSpec((None, seq, LANES), lambda b, h, k=k: (b, 0, h + nh * k))

    def cpart(k):
        return pl.BlockSpec((CONV_W, LANES), lambda b, h, k=k: (0, h + nh * k))

    pad = lambda t: jnp.pad(t.astype(F32), (0, LANES - nh)).reshape(1, LANES)
    small = pl.BlockSpec((1, LANES), lambda b, h: (0, 0))
    return pl.pallas_call(
        functools.partial(_gdn_kernel, seq=seq),
        grid=(bsz, nh),
        in_specs=[part(0), part(1), part(2), part(3),
                  pl.BlockSpec((None, seq, SM_W), lambda b, h: (b, 0, 0)),
                  cpart(0), cpart(1), cpart(2), small, small, small],
        out_specs=pl.BlockSpec((None, seq, LANES), lambda b, h: (b, 0, h)),
        out_shape=jax.ShapeDtypeStruct((bsz, seq, GD_W), BF16),
        scratch_shapes=[pltpu.VMEM((nchunks * LANES, LANES), BF16), pltpu.VMEM((nchunks * LANES, LANES), F32),
                        pltpu.VMEM((seq, LANES), BF16), pltpu.VMEM((seq, LANES), F32),
                        pltpu.VMEM((nchunks * 8, LANES), F32)],
        compiler_params=_params(("parallel", "parallel"), 40),
        name="gdn",
    )(p_gd, p_gd, p_gd, p_gd, p_sm, conv_w, conv_w, conv_w, pad(a_log), pad(dt_bias),
      norm_w.reshape(1, LANES))


def _hgrn_gdn(p_hg, lb, hg_norm, p_gd, p_sm, conv_w, a_log, dt_bias, gd_norm):
    bsz, seq, _ = p_hg.shape
    nh = HG_HEADS
    nchunks = seq // GD_CHUNK
    assert HG_HEADS == GD_HEADS and seq % (GD_CHUNK * GD_GROUP) == 0
    assert (GD_GROUP * GD_CHUNK) % (HG_ROWS * HG_GROUP) == 0 and len(_FUSED_ORDER) == 11 + 6 * 4

    def part(k):
        return pl.BlockSpec((None, seq, LANES), lambda b, h, k=k: (b, 0, h + nh * k))

    def cpart(k):
        return pl.BlockSpec((CONV_W, LANES), lambda b, h, k=k: (0, h + nh * k))

    pad = lambda t: jnp.pad(t.astype(F32), (0, LANES - nh)).reshape(1, LANES)
    small = pl.BlockSpec((1, LANES), lambda b, h: (0, 0))
    head_out = pl.BlockSpec((None, seq, LANES), lambda b, h: (b, 0, h))
    return pl.pallas_call(
        functools.partial(_hgrn_gdn_kernel, seq=seq),
        grid=(bsz, nh),
        in_specs=[part(0), part(1), part(2), part(3), pl.BlockSpec((1, LANES), lambda b, h: (0, h)), small,
                  part(0), part(1), part(2), part(3),
                  pl.BlockSpec((None, seq, SM_W), lambda b, h: (b, 0, 0)),
                  cpart(0), cpart(1), cpart(2), small, small, small],
        out_specs=[head_out, head_out],
        out_shape=[jax.ShapeDtypeStruct((bsz, seq, HG_W), BF16), jax.ShapeDtypeStruct((bsz, seq, GD_W), BF16)],
        scratch_shapes=[pltpu.VMEM((nchunks * LANES, LANES), BF16), pltpu.VMEM((nchunks * LANES, LANES), F32),
                        pltpu.VMEM((seq, LANES), BF16), pltpu.VMEM((seq, LANES), F32),
                        pltpu.VMEM((nchunks * 8, LANES), F32)],
        compiler_params=_params(("parallel", "parallel"), 48),
        name="hgrn_gdn",
    )(p_hg, p_hg, p_hg, p_hg, lb.reshape(1, HG_W), hg_norm.reshape(1, LANES),
      p_gd, p_gd, p_gd, p_gd, p_sm, conv_w, conv_w, conv_w, pad(a_log), pad(dt_bias),
      gd_norm.reshape(1, LANES))


NSA_TQ = 256
NSA_SQ = 128
NSA_TK = 256
ROPE_ROWS = 256


def _gelu_tanh(x):
    return 0.5 * x * (1.0 + jnp.tanh(0.7978845608028654 * (x + 0.044715 * x * x * x)))


def _nsa_prep_kernel(kc_ref, vc_ref, ks_ref, vs_ref, kw_ref, vw_ref, cos_ref, sin_ref,
                     posk_ref, w1k_ref, w2k_ref, posv_ref, w1v_ref, w2v_ref,
                     kcmp_ref, vcmp_ref, ksr_ref, vsb_ref, kwr_ref, vwb_ref, kcr_s, *, seq):
    ncmp = seq // CMP_STRIDE
    per = CMP_LEN // CMP_STRIDE

    def rope_rows(i, carry):
        r0 = pl.multiple_of(i * ROPE_ROWS, ROPE_ROWS)
        rs = pl.ds(r0, ROPE_ROWS)
        cosf = cos_ref[rs, :]
        sins = sin_ref[rs, :]

        def rope(x):
            return x * cosf + pltpu.roll(x, HEAD_DIM // 2, 1) * sins

        kcr_s[rs, :] = rope(kc_ref[rs, :])
        ksr_ref[rs, :] = rope(ks_ref[rs, :]).astype(BF16)
        kwr_ref[rs, :] = rope(kw_ref[rs, :]).astype(BF16)
        vsb_ref[:, rs] = jnp.transpose(vs_ref[rs, :]).astype(BF16)
        vwb_ref[:, rs] = jnp.transpose(vw_ref[rs, :]).astype(BF16)
        return carry

    lax.fori_loop(0, seq // ROPE_ROWS, rope_rows, 0)

    def compress(src_ref, pos_ref, w1_ref, w2_ref, out_ref, transpose_out):
        parts = [jnp.zeros((ncmp, CMP_HIDDEN), F32) for _ in range(per)]
        for j in range(CMP_STRIDE):
            xj = src_ref[pl.ds(j, ncmp, stride=CMP_STRIDE), :]
            for p in range(per):
                jj = p * CMP_STRIDE + j
                xp = (xj + pos_ref[jj:jj + 1, :]).astype(BF16)
                parts[p] = parts[p] + jnp.dot(xp, w1_ref[jj * HEAD_DIM:(jj + 1) * HEAD_DIM, :],
                                              preferred_element_type=F32)
        pre = parts[0]
        for p in range(1, per):
            pre = pre + pltpu.roll(parts[p], ncmp - p, 0)
        hid = _gelu_tanh(pre)
        out = jnp.dot(hid.astype(BF16), w2_ref[...], preferred_element_type=F32)
        out_ref[...] = (jnp.transpose(out) if transpose_out else out).astype(BF16)

    compress(kcr_s, posk_ref, w1k_ref, w2k_ref, kcmp_ref, False)
    compress(vc_ref, posv_ref, w1v_ref, w2v_ref, vcmp_ref, True)


def _nsa_prep(p_nsa, cosf, sins, posk, w1k, w2k, posv, w1v, w2v):
    bsz, seq, _ = p_nsa.shape
    ng = NSA_KV
    ncmp = seq // CMP_STRIDE
    q_blocks = NSA_W // LANES

    def part(k):
        return pl.BlockSpec((None, seq, LANES), lambda b, g, k=k: (b, 0, q_blocks + ng * k + g))

    def whole(shape):
        return pl.BlockSpec(shape, lambda b, g: (0,) * len(shape))

    seq_out = pl.BlockSpec((None, None, seq, LANES), lambda b, g: (b, g, 0, 0))
    seq_out_t = pl.BlockSpec((None, None, LANES, seq), lambda b, g: (b, g, 0, 0))
    cmp_out = pl.BlockSpec((None, None, ncmp, LANES), lambda b, g: (b, g, 0, 0))
    seq_shape = jax.ShapeDtypeStruct((bsz, ng, seq, LANES), BF16)
    seq_shape_t = jax.ShapeDtypeStruct((bsz, ng, LANES, seq), BF16)
    cmp_shape = jax.ShapeDtypeStruct((bsz, ng, ncmp, LANES), BF16)
    return pl.pallas_call(
        functools.partial(_nsa_prep_kernel, seq=seq),
        grid=(bsz, ng),
        in_specs=[part(0), part(1), part(2), part(3), part(4), part(5),
                  whole((seq, LANES)), whole((seq, LANES)),
                  whole((CMP_LEN, HEAD_DIM)), whole((CMP_LEN * HEAD_DIM, CMP_HIDDEN)), whole((CMP_HIDDEN, HEAD_DIM)),
                  whole((CMP_LEN, HEAD_DIM)), whole((CMP_LEN * HEAD_DIM, CMP_HIDDEN)), whole((CMP_HIDDEN, HEAD_DIM))],
        out_specs=[cmp_out, cmp_out, seq_out, seq_out_t, seq_out, seq_out_t],
        out_shape=[cmp_shape, cmp_shape, seq_shape, seq_shape_t, seq_shape, seq_shape_t],
        scratch_shapes=[pltpu.VMEM((seq, LANES), F32)],
        compiler_params=_params(("parallel", "parallel"), 48),
        name="nsa_prep",
    )(p_nsa, p_nsa, p_nsa, p_nsa, p_nsa, p_nsa, cosf, sins, posk, w1k, w2k, posv, w1v, w2v)


def _nsa_attn_kernel(q_ref, cos_ref, sin_ref, kcmp_ref, vcmp_ref, ks_ref, vs_ref, kw_ref, vw_ref,
                     sm_ref, o_ref, sel_s, *, seq):
    tq, sq, tk, hpg = NSA_TQ, NSA_SQ, NSA_TK, NSA_HPG
    streams = range(tq // sq)
    cols = hpg * sq
    scale = HEAD_DIM ** -0.5
    g = pl.program_id(1)
    t0 = pl.program_id(2) * tq
    ncmp = seq // CMP_STRIDE
    nslc = seq // SLC_LEN
    kcmp = kcmp_ref[...]
    vcmp_t = vcmp_ref[...]
    heads = [slice(hh * sq, (hh + 1) * sq) for hh in range(hpg)]

    def rope_q_t(st):
        rs = slice(st * sq, (st + 1) * sq)
        cosf = cos_ref[rs, :] * (scale * LOG2E)
        sins = sin_ref[rs, :] * (scale * LOG2E)
        out = []
        for hh in range(hpg):
            x = q_ref[rs, hh * HEAD_DIM:(hh + 1) * HEAD_DIM]
            out.append(jnp.transpose(x * cosf + pltpu.roll(x, HEAD_DIM // 2, 1) * sins).astype(BF16))
        return jnp.concatenate(out, axis=1)

    qts = [rope_q_t(st) for st in streams]
    t_ls = [t0 + st * sq + lax.broadcasted_iota(jnp.int32, (1, sq), 1) for st in streams]

    n_sub = lax.broadcasted_iota(jnp.int32, (ncmp, 1), 0)
    cmasks = [(n_sub * CMP_STRIDE + (CMP_LEN - 1)) <= t_l for t_l in t_ls]
    ss = [jnp.dot(kcmp, qt, preferred_element_type=F32) for qt in qts]
    ss = [[jnp.where(cm, s[:, hd], NEG) for hd in heads] for cm, s in zip(cmasks, ss)]
    es = [[jnp.exp2(s - jnp.max(s, axis=0, keepdims=True)) for s in sh] for sh in ss]
    p_cmps = [[jnp.where(cm, e * (1.0 / jnp.sum(e, axis=0, keepdims=True)), 0.0) for e in eh]
              for cm, eh in zip(cmasks, es)]
    o_cmps = [jnp.dot(vcmp_t, jnp.concatenate(ph, axis=1).astype(BF16), preferred_element_type=F32)
              for ph in p_cmps]

    j_i = lax.broadcasted_iota(jnp.int32, (nslc, ncmp), 0)
    n_i = lax.broadcasted_iota(jnp.int32, (nslc, ncmp), 1)
    ovl = (jnp.minimum(n_i * CMP_STRIDE + CMP_LEN, j_i * SLC_LEN + SLC_LEN)
           - jnp.maximum(n_i * CMP_STRIDE, j_i * SLC_LEN))
    ovl = jnp.maximum(ovl, 0).astype(F32) * (1.0 / CMP_LEN)
    imps = [jnp.dot(ovl, functools.reduce(lambda a, b: a + b, ph), precision=HI, preferred_element_type=F32)
            for ph in p_cmps]
    blk = lax.broadcasted_iota(jnp.int32, (nslc, 1), 0)
    imps = [jnp.where((blk == 0) | (blk == t_l // SLC_LEN) | (blk == t_l // SLC_LEN - 1), jnp.inf, imp)
            for imp, t_l in zip(imps, t_ls)]
    imps = [jnp.where(blk * SLC_LEN <= t_l, imp, -jnp.inf) for imp, t_l in zip(imps, t_ls)]
    ranks = [jnp.zeros((nslc, sq), F32) for _ in streams]
    for i in range(nslc):
        ranks = [rank + jnp.where((imp[i:i + 1, :] > imp) | ((imp[i:i + 1, :] == imp) & (blk > i)), 1.0, 0.0)
                 for rank, imp in zip(ranks, imps)]
    for st, rank in zip(streams, ranks):
        sel_s[st * nslc:(st + 1) * nslc, :] = jnp.where(rank < float(min(SLC_TOP, nslc)), 1.0, 0.0)

    key_sub = lax.broadcasted_iota(jnp.int32, (tk, 1), 0)
    slc_sub = lax.broadcasted_iota(jnp.int32, (SLC_LEN, 1), 0)

    units = [(st, hd) for st in streams for hd in heads]

    def attend(kb, carries, jobs):
        k0 = pl.multiple_of(kb * tk, tk)
        work = []
        for j, (k_ref, vt_ref, bias_fn) in enumerate(jobs):
            kblk = k_ref[pl.ds(k0, tk), :]
            vblk_t = vt_ref[:, pl.ds(k0, tk)]
            biases = bias_fn(k0)
            work += [(j, u, kblk, vblk_t, biases[st], qts[st][:, hd]) for u, (st, hd) in enumerate(units)]
        carries = [list(c) for c in carries]

        def scores(w):
            return jnp.dot(work[w][2], work[w][5], preferred_element_type=F32)

        s_next = scores(0)
        for w, (j, u, _, vblk_t, bias, _) in enumerate(work):
            sh = s_next + bias
            if w + 1 < len(work):
                s_next = scores(w + 1)
            m_i, l_i, acc = carries[j][u]
            m_new = jnp.maximum(m_i, jnp.max(sh, axis=0, keepdims=True))
            alpha = jnp.exp2(m_i - m_new)
            p = jnp.exp2(sh - m_new)
            l_new = alpha * l_i + jnp.sum(p, axis=0, keepdims=True)
            pv = jnp.dot(vblk_t, p.astype(BF16), preferred_element_type=F32)
            carries[j][u] = (m_new, l_new, alpha * acc + pv)
        return tuple(tuple(c) for c in carries)

    init = tuple((jnp.full((1, sq), NEG, F32), jnp.zeros((1, sq), F32), jnp.zeros((HEAD_DIM, sq), F32))
                 for _ in units)

    def slc_bias(k0):
        out = []
        for st, t_l in zip(streams, t_ls):
            pieces = []
            for i in range(tk // SLC_LEN):
                picked = sel_s[pl.ds(st * nslc + k0 // SLC_LEN + i, 1), :]
                kpos = k0 + i * SLC_LEN + slc_sub
                pieces.append(jnp.where((picked > 0.5) & (kpos <= t_l), 0.0, NEG))
            out.append(jnp.concatenate(pieces, axis=0))
        return out

    def win_bias(k0):
        kpos = k0 + key_sub
        return [jnp.where((kpos <= t_l) & (kpos > t_l - WINDOW), 0.0, NEG) for t_l in t_ls]

    n_blocks = (t0 + tq + tk - 1) // tk
    win_lo = jnp.maximum(t0 - WINDOW + 1, 0) // tk
    slc_job = (ks_ref, vs_ref, slc_bias)
    win_job = (kw_ref, vw_ref, win_bias)
    (slc,) = lax.fori_loop(0, win_lo, lambda kb, c: attend(kb, c, [slc_job]), (init,))
    slc, win = lax.fori_loop(win_lo, n_blocks, lambda kb, c: attend(kb, c, [slc_job, win_job]), (slc, init))
    o_slcs = [acc * (1.0 / l) for _, l, acc in slc]
    o_wins = [acc * (1.0 / l) for _, l, acc in win]

    gate_row = lax.broadcasted_iota(jnp.int32, (SM_W, sq), 0)
    for st in streams:
        rs = slice(st * sq, (st + 1) * sq)
        sm_t = jnp.transpose(sm_ref[rs, :])

        def gate(col):
            return jax.nn.sigmoid(jnp.sum(jnp.where(gate_row == col, sm_t, 0.0), axis=0, keepdims=True))

        for hh, hd in enumerate(heads):
            col = GATE_COL0 + (g * hpg + hh) * 3
            u = st * hpg + hh
            o_t = (gate(col) * o_cmps[st][:, hd] + gate(col + 1) * o_slcs[u]
                   + gate(col + 2) * o_wins[u])
            o_ref[rs, hh * HEAD_DIM:(hh + 1) * HEAD_DIM] = jnp.transpose(o_t).astype(o_ref.dtype)


def _nsa_attn(p_nsa, p_sm, cosf, sins, kcmp, vcmp, ksr, vsb, kwr, vwb):
    bsz, seq, _ = p_nsa.shape
    ng = NSA_KV
    tq = NSA_TQ
    assert seq % tq == 0 and seq // CMP_STRIDE == LANES
    ncmp = seq // CMP_STRIDE
    gw = NSA_HPG * HEAD_DIM
    per_group = lambda rows: pl.BlockSpec((None, None, rows, LANES), lambda b, g, t: (b, g, 0, 0))
    per_group_t = pl.BlockSpec((None, None, LANES, seq), lambda b, g, t: (b, g, 0, 0))
    return pl.pallas_call(
        functools.partial(_nsa_attn_kernel, seq=seq),
        grid=(bsz, ng, seq // tq),
        in_specs=[pl.BlockSpec((None, tq, gw), lambda b, g, t: (b, t, g)),
                  pl.BlockSpec((tq, LANES), lambda b, g, t: (t, 0)),
                  pl.BlockSpec((tq, LANES), lambda b, g, t: (t, 0)),
                  per_group(ncmp), per_group(ncmp),
                  per_group(seq), per_group_t, per_group(seq), per_group_t,
                  pl.BlockSpec((None, tq, SM_W), lambda b, g, t: (b, t, 0))],
        out_specs=pl.BlockSpec((None, tq, gw), lambda b, g, t: (b, t, g)),
        out_shape=jax.ShapeDtypeStruct((bsz, seq, NSA_W), BF16),
        scratch_shapes=[pltpu.VMEM((tq // NSA_SQ * (seq // SLC_LEN), NSA_SQ), F32)],
        compiler_params=_params(("parallel", "parallel", "arbitrary"), 40),
        name="nsa_attn",
    )(p_nsa, cosf, sins, kcmp, vcmp, ksr, vsb, kwr, vwb, p_sm)


def _rope_tables(seq):
    inv = ROPE_THETA ** (-jnp.arange(0, HEAD_DIM, 2, dtype=F32) / HEAD_DIM)
    ang = jnp.arange(seq, dtype=F32)[:, None] * inv[None, :]
    cos, sin = jnp.cos(ang), jnp.sin(ang)
    return jnp.concatenate([cos, cos], axis=-1), jnp.concatenate([-sin, sin], axis=-1)


def _mixers(p_hg, p_gd, p_nsa, p_sm, lb, hg_norm, conv_w, a_log, dt_bias, gd_norm,
            posk, w1k, w2k, posv, w1v, w2v, cosf, sins):
    o_a, o_b = _hgrn_gdn(p_hg, lb, hg_norm, p_gd, p_sm, conv_w, a_log, dt_bias, gd_norm)
    prep = _nsa_prep(p_nsa, cosf, sins, posk, w1k.astype(BF16), w2k.astype(BF16),
                     posv, w1v.astype(BF16), w2v.astype(BF16))
    o_c = _nsa_attn(p_nsa, p_sm, cosf, sins, *prep)
    return o_a, o_b, o_c


def kernel(x, attn_norm, w_in, w_out, ffn_norm, w_gate, w_up, w_down, final_norm, hgrn_lb_logits, hgrn_out_norm, gdn_conv, gdn_A_log, gdn_dt_bias, gdn_out_norm, cmp_pos_k, cmp_w1_k, cmp_w2_k, cmp_pos_v, cmp_w1_v, cmp_w2_v):
    bsz, seq, d = x.shape
    depth = w_in.shape[0]
    n = bsz * seq
    cosf, sins = _rope_tables(seq)
    lb_p = jax.nn.softmax(hgrn_lb_logits.astype(F32), axis=0)
    lb_all = jnp.concatenate([jnp.zeros_like(lb_p[:1]), jnp.cumsum(lb_p, axis=0)[:-1]], axis=0)

    c_gd = 4 * HG_W
    c_ab = c_gd + 4 * GD_W
    c_nsa = c_ab + 2 * GD_HEADS
    c_gate = c_nsa + NSA_W + 6 * NSA_KV_W
    n_gate = NSA_HEADS * 3

    w_t = jnp.swapaxes(w_in, 1, 2)
    w_nsa = w_t[:, c_nsa:c_gate, :]
    w_sm = jnp.concatenate(
        [w_t[:, c_ab:c_nsa, :], w_t[:, c_gate:, :],
         jnp.zeros((depth, SM_W - 2 * GD_HEADS - n_gate, d), F32)], axis=1)
    w_dn = w_down.astype(BF16)

    xf = x.reshape(n, d)
    xw, ss = _norm_inputs(xf, attn_norm[0])
    for l in range(depth):
        p_hg = _matmul(xw, ss, w_t, l, 0, c_gd, transposed=True).reshape(bsz, seq, -1)
        p_gd = _matmul(xw, ss, w_t, l, c_gd, c_ab - c_gd, transposed=True).reshape(bsz, seq, -1)
        p_nsa = _matmul(xw, ss, w_nsa, l, transposed=True).reshape(bsz, seq, -1)
        p_sm = _matmul(xw, ss, w_sm, l, transposed=True).reshape(bsz, seq, -1)
        mix = _mixers(p_hg, p_gd, p_nsa, p_sm, lb_all[l], hgrn_out_norm[l], gdn_conv[l],
                      gdn_A_log[l], gdn_dt_bias[l], gdn_out_norm[l],
                      cmp_pos_k[l], cmp_w1_k[l], cmp_w2_k[l], cmp_pos_v[l], cmp_w1_v[l], cmp_w2_v[l],
                      cosf, sins)
        xf, xw, ss = _out_proj([o.reshape(n, o.shape[-1]) for o in mix], w_out, l, xf, ffn_norm[l])
        act = _gate_up(xw, ss, w_gate, w_up, l)
        next_norm = attn_norm[l + 1] if l + 1 < depth else final_norm
        xf, xw, ss = _matmul_ktiled_res(act, w_dn, l, xf, next_norm)
    return _rmsnorm(xf, final_norm, F32).reshape(bsz, seq, d)
```

```python
import functools

import jax
import jax.numpy as jnp
from jax import lax
from jax.experimental import pallas as pl
from jax.experimental.pallas import tpu as pltpu

F32 = jnp.float32
BF16 = jnp.bfloat16

HEAD_DIM = 128
HG_HEADS = 8
GD_HEADS = 8
CONV_W = 4
NSA_HEADS = 16
NSA_KV = 4
NSA_HPG = NSA_HEADS // NSA_KV
CMP_LEN = 32
CMP_STRIDE = 16
CMP_HIDDEN = 256
SLC_LEN = 64
SLC_TOP = 16
WINDOW = 512
ROPE_THETA = 10000.0
NORM_EPS = 1e-6
LB_FLOOR = 1e-30
NEG = -1e30
LOG2E = 1.4426950408889634
FFN_HIDDEN = 11008

HG_W = HG_HEADS * HEAD_DIM
GD_W = GD_HEADS * HEAD_DIM
NSA_W = NSA_HEADS * HEAD_DIM
NSA_KV_W = NSA_KV * HEAD_DIM
SM_W = 128
GATE_COL0 = 2 * GD_HEADS

LANES = 128
MIB = 1 << 20

MM_TM, MM_TN = 1024, 512
FFN_TN = 256
DOWN_TK = 5504
NORM_ROWS = 256
VMEM_MATMUL_MIB = 56
VMEM_DOWN_MIB = 48
VMEM_ROWWISE_MIB = 40
VMEM_MIXER_MIB = 48

NT = (((1,), (1,)), ((), ()))
TN = (((0,), (0,)), ((), ()))


def _params(sem, vmem_mib):
    return pltpu.CompilerParams(dimension_semantics=sem, vmem_limit_bytes=vmem_mib * MIB)


def _split2(x):
    hi = x.astype(BF16)
    return hi, (x - hi.astype(F32)).astype(BF16)


def _dot_x3(a, b):
    ah, al = _split2(a)
    bh, bl = _split2(b)
    d = lambda x, y: jnp.dot(x, y, preferred_element_type=F32)
    return d(ah, bh) + (d(ah, bl) + d(al, bh))


def _dot_exact_lhs(a, b):
    hi = b.astype(BF16)
    r = b - hi.astype(F32)
    mid = r.astype(BF16)
    lo = (r - mid.astype(F32)).astype(BF16)
    d = lambda y: jnp.dot(a, y, preferred_element_type=F32)
    return d(hi) + (d(mid) + d(lo))


def _silu(x):
    return x * jax.nn.sigmoid(x)


def _softplus(x):
    return jnp.maximum(x, 0.0) + jnp.log1p(jnp.exp(-jnp.abs(x)))


def _lane_pick(x, lane_ids, idx):
    return jnp.sum(jnp.where(lane_ids == idx, x, 0.0), axis=-1, keepdims=True)


def _rms_kernel(x_ref, w_ref, o_ref):
    x = x_ref[...]
    ms = jnp.mean(x * x, axis=-1, keepdims=True)
    o_ref[...] = (x * lax.rsqrt(ms + NORM_EPS) * w_ref[...]).astype(o_ref.dtype)


def _rmsnorm(x, w, out_dtype, rows=NORM_ROWS):
    n, d = x.shape
    return pl.pallas_call(
        _rms_kernel,
        grid=(n // rows,),
        in_specs=[pl.BlockSpec((rows, d), lambda i: (i, 0)),
                  pl.BlockSpec((1, d), lambda i: (0, 0))],
        out_specs=pl.BlockSpec((rows, d), lambda i: (i, 0)),
        out_shape=jax.ShapeDtypeStruct((n, d), out_dtype),
        compiler_params=_params(("parallel",), VMEM_ROWWISE_MIB),
        name="rmsnorm",
    )(x, w.reshape(1, d))


def _rinv(ss_ref, d):
    return lax.rsqrt(ss_ref[...] * (1.0 / d) + NORM_EPS)


def _emit_norm_inputs(x, nw_ref, xw_ref, ss_ref, first):
    xw_ref[...] = (x * nw_ref[...]).astype(xw_ref.dtype)
    part = jnp.sum(x * x, axis=-1, keepdims=True)

    @pl.when(first)
    def _():
        ss_ref[...] = part

    @pl.when(jnp.logical_not(first))
    def _():
        ss_ref[...] += part


def _norm_inputs_kernel(x_ref, w_ref, xw_ref, ss_ref):
    x = x_ref[...]
    xw_ref[...] = (x * w_ref[...]).astype(xw_ref.dtype)
    ss_ref[...] = jnp.sum(x * x, axis=-1, keepdims=True)


def _norm_inputs(x, w, rows=NORM_ROWS):
    n, d = x.shape
    return pl.pallas_call(
        _norm_inputs_kernel,
        grid=(n // rows,),
        in_specs=[pl.BlockSpec((rows, d), lambda i: (i, 0)),
                  pl.BlockSpec((1, d), lambda i: (0, 0))],
        out_specs=[pl.BlockSpec((rows, d), lambda i: (i, 0)), pl.BlockSpec((rows, 1), lambda i: (i, 0))],
        out_shape=[jax.ShapeDtypeStruct((n, d), BF16), jax.ShapeDtypeStruct((n, 1), F32)],
        compiler_params=_params(("parallel",), VMEM_ROWWISE_MIB),
        name="norm_inputs",
    )(x, w.reshape(1, d))


def _mm_nt_kernel(a_ref, b_ref, ss_ref, o_ref):
    acc = lax.dot_general(a_ref[...], b_ref[...].astype(BF16), NT, preferred_element_type=F32)
    o_ref[...] = _rinv(ss_ref, a_ref.shape[1]) * acc


def _matmul_nt(xw, ss, w_t, layer, row0=0, n=None, tm=MM_TM, tn=MM_TN):
    m, k = xw.shape
    n = w_t.shape[1] if n is None else n
    tn = min(tn, n)
    assert m % tm == 0 and n % tn == 0 and row0 % tn == 0
    rb = row0 // tn
    return pl.pallas_call(
        _mm_nt_kernel,
        grid=(m // tm, n // tn),
        in_specs=[pl.BlockSpec((tm, k), lambda i, j: (i, 0)),
                  pl.BlockSpec((None, tn, k), lambda i, j: (layer, rb + j, 0)),
                  pl.BlockSpec((tm, 1), lambda i, j: (i, 0))],
        out_specs=pl.BlockSpec((tm, tn), lambda i, j: (i, j)),
        out_shape=jax.ShapeDtypeStruct((m, n), F32),
        compiler_params=_params(("parallel", "parallel"), VMEM_MATMUL_MIB),
        name="matmul",
    )(xw, w_t, ss)


def _residual_outputs(m, n, tm, tn, index):
    specs = [pl.BlockSpec((tm, tn), index), pl.BlockSpec((tm, tn), index),
             pl.BlockSpec((tm, 1), lambda i, *_: (i, 0))]
    shapes = [jax.ShapeDtypeStruct((m, n), F32), jax.ShapeDtypeStruct((m, n), BF16),
              jax.ShapeDtypeStruct((m, 1), F32)]
    return specs, shapes


def _out_proj_kernel(a_ref, b_ref, c_ref, w_ref, r_ref, nw_ref, o_ref, xw_ref, ss_ref):
    ka, kb = a_ref.shape[1], b_ref.shape[1]
    acc = r_ref[...] + jnp.dot(a_ref[...], w_ref[0:ka, :].astype(BF16), preferred_element_type=F32)
    acc = acc + jnp.dot(b_ref[...], w_ref[ka:ka + kb, :].astype(BF16), preferred_element_type=F32)
    x = acc + jnp.dot(c_ref[...], w_ref[ka + kb:, :].astype(BF16), preferred_element_type=F32)
    o_ref[...] = x
    _emit_norm_inputs(x, nw_ref, xw_ref, ss_ref, pl.program_id(1) == 0)


def _out_proj(parts, w, layer, res, next_norm_w, tm=MM_TM, tn=MM_TN):
    m, n = res.shape
    k = w.shape[1]
    tile = lambda i, j: (i, j)
    out_specs, out_shape = _residual_outputs(m, n, tm, tn, tile)
    return pl.pallas_call(
        _out_proj_kernel,
        grid=(m // tm, n // tn),
        in_specs=[pl.BlockSpec((tm, p.shape[1]), lambda i, j: (i, 0)) for p in parts]
        + [pl.BlockSpec((None, k, tn), lambda i, j: (layer, 0, j)),
           pl.BlockSpec((tm, tn), tile), pl.BlockSpec((1, tn), lambda i, j: (0, j))],
        out_specs=out_specs,
        out_shape=out_shape,
        compiler_params=_params(("parallel", "arbitrary"), VMEM_MATMUL_MIB),
        name="out_proj",
    )(*parts, w, res, next_norm_w.reshape(1, n))


def _mm_acc_res_kernel(a_ref, b_ref, r_ref, nw_ref, o_ref, xw_ref, ss_ref, acc_ref, *, nk):
    k = pl.program_id(2)

    @pl.when(k == 0)
    def _():
        acc_ref[...] = r_ref[...]

    acc_ref[...] += jnp.dot(a_ref[...], b_ref[...], preferred_element_type=F32)

    @pl.when(k == nk - 1)
    def _():
        x = acc_ref[...]
        o_ref[...] = x
        _emit_norm_inputs(x, nw_ref, xw_ref, ss_ref, pl.program_id(1) == 0)


def _matmul_ktiled_res(a, b, layer, res, next_norm_w, tm=MM_TM, tn=MM_TN, tk=DOWN_TK):
    m, k = a.shape
    n = b.shape[2]
    nk = k // tk
    tile = lambda i, j, kk: (i, j)
    out_specs, out_shape = _residual_outputs(m, n, tm, tn, tile)
    return pl.pallas_call(
        functools.partial(_mm_acc_res_kernel, nk=nk),
        grid=(m // tm, n // tn, nk),
        in_specs=[pl.BlockSpec((tm, tk), lambda i, j, kk: (i, kk)),
                  pl.BlockSpec((None, tk, tn), lambda i, j, kk: (layer, kk, j)),
                  pl.BlockSpec((tm, tn), tile), pl.BlockSpec((1, tn), lambda i, j, kk: (0, j))],
        out_specs=out_specs,
        out_shape=out_shape,
        scratch_shapes=[pltpu.VMEM((tm, tn), F32)],
        compiler_params=_params(("parallel", "arbitrary", "arbitrary"), VMEM_DOWN_MIB),
        name="matmul_down",
    )(a, b, res, next_norm_w.reshape(1, n))


def _gate_up_kernel(a_ref, wg_ref, wu_ref, ss_ref, o_ref):
    a = a_ref[...]
    rinv = _rinv(ss_ref, a_ref.shape[1])
    g = rinv * jnp.dot(a, wg_ref[...].astype(BF16), preferred_element_type=F32)
    u = rinv * jnp.dot(a, wu_ref[...].astype(BF16), preferred_element_type=F32)
    o_ref[...] = (_silu(g) * u).astype(o_ref.dtype)


def _gate_up(xw, ss, wg, wu, layer, tm=MM_TM, tn=FFN_TN):
    m, k = xw.shape
    n = wg.shape[2]
    return pl.pallas_call(
        _gate_up_kernel,
        grid=(m // tm, n // tn),
        in_specs=[pl.BlockSpec((tm, k), lambda i, j: (i, 0)),
                  pl.BlockSpec((None, k, tn), lambda i, j: (layer, 0, j)),
                  pl.BlockSpec((None, k, tn), lambda i, j: (layer, 0, j)),
                  pl.BlockSpec((tm, 1), lambda i, j: (i, 0))],
        out_specs=pl.BlockSpec((tm, tn), lambda i, j: (i, j)),
        out_shape=jax.ShapeDtypeStruct((m, n), BF16),
        compiler_params=_params(("parallel", "parallel"), VMEM_MATMUL_MIB),
        name="ffn_gate_up",
    )(xw, wg, wu, ss)


HG_ROWS = 64
HG_SUB = 16
HG_GROUP = 2


def _hgrn_phases(q_ref, f_ref, i_ref, g_ref, lb_ref, nw_ref, o_ref):
    rows, sub = HG_ROWS, HG_SUB
    row_id = lax.broadcasted_iota(jnp.int32, (rows, LANES), 0)
    sub_id = row_id & (sub - 1)
    r2 = lax.broadcasted_iota(jnp.int32, (rows, rows), 0)
    c2 = lax.broadcasted_iota(jnp.int32, (rows, rows), 1)
    tri = jnp.where(((r2 // sub) == (c2 // sub)) & (c2 <= r2), 1.0, 0.0).astype(BF16)
    log_lb = jnp.log(jnp.maximum(lb_ref[...], LB_FLOOR))
    nw = nw_ref[...]
    nsub = rows // sub

    def features(c):
        r0 = pl.multiple_of(c * rows, rows)
        q = _silu(q_ref[pl.ds(r0, rows), :])
        fp = f_ref[pl.ds(r0, rows), :]
        v = i_ref[pl.ds(r0, rows), :]
        logf = jnp.minimum(-_softplus(-fp) + _softplus(log_lb - fp), 0.0)
        return q, 1.0 - jnp.exp(logf), v, logf

    def phases(i, box):
        cs = [i * HG_GROUP + j for j in range(HG_GROUP)]
        d_ = {}

        def load():
            d_["feats"] = [features(c) for c in cs]

        def cumsum():
            d_["bs"] = [_dot_exact_lhs(tri, f[3]) * LOG2E for f in d_["feats"]]

        def intra():
            feats, bs = d_["feats"], d_["bs"]
            accs = [jnp.sum(q * kf, axis=-1, keepdims=True) * v for q, kf, v, _ in feats]
            for d in range(1, sub):
                prods = [jnp.where(sub_id >= d, q * pltpu.roll(kf, d, 0)
                                   * jnp.exp2(jnp.minimum(b - pltpu.roll(b, d, 0), 0.0)), 0.0)
                         for (q, kf, _, _), b in zip(feats, bs)]
                accs = [acc + jnp.sum(p, axis=-1, keepdims=True) * pltpu.roll(f[2], d, 0)
                        for acc, p, f in zip(accs, prods, feats)]
            d_["accs"] = accs

        def products():
            pieces = []
            for (q, kf, v, _), b in zip(d_["feats"], d_["bs"]):
                for j in range(nsub):
                    sl = slice(j * sub, (j + 1) * sub)
                    bj = b[sl]
                    bl = bj[sub - 1:sub]
                    kh = (kf[sl] * jnp.exp2(bl - bj)).astype(BF16)
                    vk = lax.dot_general(v[sl].astype(BF16), kh, TN, preferred_element_type=F32)
                    pieces.append(((q[sl] * jnp.exp2(bj)).astype(BF16), jnp.exp2(bl), vk))
            d_["pieces"] = pieces

        def chain():
            state, inter = box["s"], []
            for qt, decay, vk in d_["pieces"]:
                inter.append(lax.dot_general(qt, state.astype(BF16), NT, preferred_element_type=F32))
                state = decay * state + vk
            box["s"], d_["inter"] = state, inter

        def emit():
            for n, c in enumerate(cs):
                r0 = pl.multiple_of(c * rows, rows)
                o = d_["accs"][n] + jnp.concatenate(d_["inter"][n * nsub:(n + 1) * nsub], axis=0)
                ms = jnp.mean(o * o, axis=-1, keepdims=True)
                y = o * lax.rsqrt(ms + NORM_EPS) * nw
                y = y * _silu(g_ref[pl.ds(r0, rows), :])
                o_ref[pl.ds(r0, rows), :] = y.astype(o_ref.dtype)

        return [load, cumsum, intra, products, chain, emit]

    return phases


GD_CHUNK = 64
GD_GROUP = 8


def _gdn_program(q_ref, k_ref, v_ref, z_ref, sm_ref, cq_ref, ck_ref, cv_ref, al_ref, dtb_ref, nw_ref,
                 o_ref, am_s, bm_s, qp_s, op_s, dl_s):
    ch = GD_CHUNK
    h = pl.program_id(1)
    lane1 = lax.broadcasted_iota(jnp.int32, (1, LANES), 1)
    lane = lax.broadcasted_iota(jnp.int32, (ch, LANES), 1)
    neg_a = -jnp.exp(_lane_pick(al_ref[...], lane1, h))
    dt_bias = _lane_pick(dtb_ref[...], lane1, h)
    r2 = lax.broadcasted_iota(jnp.int32, (ch, ch), 0)
    c2 = lax.broadcasted_iota(jnp.int32, (ch, ch), 1)
    incl = c2 <= r2
    strict = c2 < r2
    tri = jnp.where(incl, 1.0, 0.0).astype(BF16)
    eye = jnp.where(c2 == r2, 1.0, 0.0).astype(F32)
    nw = nw_ref[...]

    def conv_silu(ref, w_ref, r0, c):
        cur = ref[pl.ds(r0, ch), :]
        prev = ref[pl.ds(jnp.maximum(r0 - 8, 0), 8), :]
        prev = jnp.where(c > 0, prev, 0.0)
        x = jnp.concatenate([prev, cur], axis=0)
        w = w_ref[...]
        y = cur * w[CONV_W - 1:CONV_W]
        for s in range(1, CONV_W):
            y = y + pltpu.roll(x, s, 0)[8:] * w[CONV_W - 1 - s:CONV_W - s]
        return _silu(y)

    def features(c):
        r0 = pl.multiple_of(c * ch, ch)
        qf = conv_silu(q_ref, cq_ref, r0, c)
        kf = conv_silu(k_ref, ck_ref, r0, c)
        vf = conv_silu(v_ref, cv_ref, r0, c)
        qf = qf * lax.rsqrt(jnp.sum(qf * qf, axis=-1, keepdims=True) + NORM_EPS) * (HEAD_DIM ** -0.5)
        kf = kf * lax.rsqrt(jnp.sum(kf * kf, axis=-1, keepdims=True) + NORM_EPS)
        sm = sm_ref[pl.ds(r0, ch), :]
        a = _lane_pick(sm, lane, h)
        beta = jax.nn.sigmoid(_lane_pick(sm, lane, h + GD_HEADS))
        g = neg_a * _softplus(a + dt_bias)
        return qf, kf, vf, beta, jnp.broadcast_to(g, (ch, LANES))

    def stages(i):
        cs = [i * GD_GROUP + j for j in range(GD_GROUP)]
        d_ = {}

        def load():
            d_["feats"] = [features(c) for c in cs]

        def cumsum():
            d_["gcs"] = [_dot_exact_lhs(tri, f[4]) for f in d_["feats"]]

        def gram():
            feats = d_["feats"]
            d_["decays"] = [jnp.where(incl, jnp.exp(jnp.minimum(gc[:, :ch] - jnp.transpose(gc)[:ch, :], 0.0)), 0.0)
                            for gc in d_["gcs"]]
            kbs = [f[1].astype(BF16) for f in feats]
            kks = [lax.dot_general(kb, kb, NT, preferred_element_type=F32) for kb in kbs]
            d_["qks"] = [lax.dot_general(f[0].astype(BF16), kb, NT, preferred_element_type=F32)
                         for f, kb in zip(feats, kbs)]
            d_["pws"] = [jnp.where(strict, -(f[3] * kk * d), 0.0) for f, kk, d in zip(feats, kks, d_["decays"])]
            d_["invs"] = [eye + pw for pw in d_["pws"]]

        def square():
            d_["pws"] = [_dot_x3(pw, pw) for pw in d_["pws"]]
            d_["invs"] = [inv + _dot_x3(inv, pw) for inv, pw in zip(d_["invs"], d_["pws"])]

        def solve():
            d_["egcs"] = [jnp.exp(gc) for gc in d_["gcs"]]
            d_["wus"] = [_dot_x3(inv, jnp.concatenate([f[1] * (f[3] * egc), f[2] * f[3]], axis=1)).astype(BF16)
                         for inv, f, egc in zip(d_["invs"], d_["feats"], d_["egcs"])]

        def maps():
            d_["qos"] = [jnp.dot(jnp.where(incl, qk * d, 0.0).astype(BF16), wu, preferred_element_type=F32)
                         for qk, d, wu in zip(d_["qks"], d_["decays"], d_["wus"])]
            d_["abs"] = [lax.dot_general((f[1] * jnp.exp(gc[ch - 1:ch] - gc)).astype(BF16), wu, TN,
                                         preferred_element_type=F32)
                         for f, gc, wu in zip(d_["feats"], d_["gcs"], d_["wus"])]

        def store():
            for c, f, gc, egc, qo, ab in zip(cs, d_["feats"], d_["gcs"], d_["egcs"], d_["qos"], d_["abs"]):
                r0 = pl.multiple_of(c * ch, ch)
                m0 = pl.multiple_of(c * LANES, LANES)
                am_s[pl.ds(m0, LANES), :] = ab[:, :LANES].astype(BF16)
                bm_s[pl.ds(m0, LANES), :] = ab[:, LANES:]
                qp_s[pl.ds(r0, ch), :] = (f[0] * egc - qo[:, :LANES]).astype(BF16)
                op_s[pl.ds(r0, ch), :] = qo[:, LANES:]
                dl_s[pl.ds(pl.multiple_of(c * 8, 8), 8), :] = jnp.broadcast_to(jnp.exp(gc[ch - 1:ch]), (8, LANES))

        return [load, cumsum, gram] + [square] * 5 + [solve, maps, store]

    def recur(c, state):
        r0 = pl.multiple_of(c * ch, ch)
        m0 = pl.multiple_of(c * LANES, LANES)
        sb = state.astype(BF16)
        o = jnp.dot(qp_s[pl.ds(r0, ch), :], sb, preferred_element_type=F32) + op_s[pl.ds(r0, ch), :]
        dl = dl_s[pl.ds(pl.multiple_of(c * 8, 8), 1), :]
        state = (dl * state - jnp.dot(am_s[pl.ds(m0, LANES), :], sb, preferred_element_type=F32)
                 + bm_s[pl.ds(m0, LANES), :])
        ms = jnp.mean(o * o, axis=-1, keepdims=True)
        y = o * lax.rsqrt(ms + NORM_EPS) * nw
        y = y * _silu(z_ref[pl.ds(r0, ch), :])
        o_ref[pl.ds(r0, ch), :] = y.astype(o_ref.dtype)
        return state

    return stages, recur


_FUSED_ORDER = ("g0", (0, 0), "g1", (0, 1), "g2", (0, 2), "g3", (0, 3), (0, 4), (0, 5), (1, 0), (1, 1),
                "g4", (1, 2), "g5", (1, 3), (1, 4), (1, 5), (2, 0), (2, 1), "g6", (2, 2), "g7",
                (2, 3), (2, 4), (2, 5), (3, 0), (3, 1), "g8", (3, 2), "g9", (3, 3), (3, 4), (3, 5), "g10")


def _hgrn_gdn_kernel(*refs, seq):
    hg_refs, gd_refs = refs[:6] + refs[17:18], refs[6:17] + refs[18:]
    phases = _hgrn_phases(*hg_refs)
    stages, recur = _gdn_program(*gd_refs)
    nchunks = seq // GD_CHUNK
    steps = GD_GROUP * GD_CHUNK // (HG_ROWS * HG_GROUP)

    ngroups = nchunks // GD_GROUP
    every = len(_FUSED_ORDER) // GD_GROUP

    def body(i, carry, recur_previous):
        box = {"s": carry[0]}
        gstate = carry[1]
        gd = stages(i)
        hg = [phases(i * steps + k, box) for k in range(steps)]
        for n, item in enumerate(_FUSED_ORDER):
            if isinstance(item, str):
                gd[int(item[1:])]()
            else:
                hg[item[0]][item[1]]()
            if recur_previous and n % every == every - 1 and n // every < GD_GROUP:
                gstate = recur((i - 1) * GD_GROUP + n // every, gstate)
        return box["s"], gstate

    zero = jnp.zeros((LANES, LANES), F32)
    carry = body(jnp.int32(0), (zero, zero), False)
    carry = lax.fori_loop(1, ngroups, lambda i, c: body(i, c, True), carry)
    lax.fori_loop(nchunks - GD_GROUP, nchunks, recur, carry[1], unroll=2)


def _hgrn_gdn(p_hg, lb, hg_norm, p_gd, p_sm, conv_w, a_log, dt_bias, gd_norm):
    bsz, seq, _ = p_hg.shape
    nh = HG_HEADS
    nchunks = seq // GD_CHUNK
    assert HG_HEADS == GD_HEADS and seq % (GD_CHUNK * GD_GROUP) == 0
    assert (GD_GROUP * GD_CHUNK) % (HG_ROWS * HG_GROUP) == 0 and len(_FUSED_ORDER) == 11 + 6 * 4

    def part(k):
        return pl.BlockSpec((None, seq, LANES), lambda b, h, k=k: (b, 0, h + nh * k))

    def cpart(k):
        return pl.BlockSpec((CONV_W, LANES), lambda b, h, k=k: (0, h + nh * k))

    pad = lambda t: jnp.pad(t.astype(F32), (0, LANES - nh)).reshape(1, LANES)
    small = pl.BlockSpec((1, LANES), lambda b, h: (0, 0))
    head_out = pl.BlockSpec((None, seq, LANES), lambda b, h: (b, 0, h))
    return pl.pallas_call(
        functools.partial(_hgrn_gdn_kernel, seq=seq),
        grid=(bsz, nh),
        in_specs=[part(0), part(1), part(2), part(3), pl.BlockSpec((1, LANES), lambda b, h: (0, h)), small,
                  part(0), part(1), part(2), part(3),
                  pl.BlockSpec((None, seq, SM_W), lambda b, h: (b, 0, 0)),
                  cpart(0), cpart(1), cpart(2), small, small, small],
        out_specs=[head_out, head_out],
        out_shape=[jax.ShapeDtypeStruct((bsz, seq, HG_W), BF16), jax.ShapeDtypeStruct((bsz, seq, GD_W), BF16)],
        scratch_shapes=[pltpu.VMEM((nchunks * LANES, LANES), BF16), pltpu.VMEM((nchunks * LANES, LANES), F32),
                        pltpu.VMEM((seq, LANES), BF16), pltpu.VMEM((seq, LANES), F32),
                        pltpu.VMEM((nchunks * 8, LANES), F32)],
        compiler_params=_params(("parallel", "parallel"), VMEM_MIXER_MIB),
        name="hgrn_gdn",
    )(p_hg, p_hg, p_hg, p_hg, lb.reshape(1, HG_W), hg_norm.reshape(1, LANES),
      p_gd, p_gd, p_gd, p_gd, p_sm, conv_w, conv_w, conv_w, pad(a_log), pad(dt_bias),
      gd_norm.reshape(1, LANES))


NSA_TQ = 256
NSA_SQ = 128
NSA_TK = 256
ROPE_ROWS = 256


def _gelu_tanh(x):
    return 0.5 * x * (1.0 + jnp.tanh(0.7978845608028654 * (x + 0.044715 * x * x * x)))


def _nsa_prep_kernel(kc_ref, vc_ref, ks_ref, vs_ref, kw_ref, vw_ref, cos_ref, sin_ref,
                     posk_ref, w1k_ref, w2k_ref, posv_ref, w1v_ref, w2v_ref,
                     kcmp_ref, vcmp_ref, ksr_ref, vsb_ref, kwr_ref, vwb_ref, kcr_s, *, seq):
    ncmp = seq // CMP_STRIDE
    per = CMP_LEN // CMP_STRIDE

    def rope_rows(i, carry):
        r0 = pl.multiple_of(i * ROPE_ROWS, ROPE_ROWS)
        rs = pl.ds(r0, ROPE_ROWS)
        cosf = cos_ref[rs, :]
        sins = sin_ref[rs, :]

        def rope(x):
            return x * cosf + pltpu.roll(x, HEAD_DIM // 2, 1) * sins

        kcr_s[rs, :] = rope(kc_ref[rs, :])
        ksr_ref[rs, :] = rope(ks_ref[rs, :]).astype(BF16)
        kwr_ref[rs, :] = rope(kw_ref[rs, :]).astype(BF16)
        vsb_ref[:, rs] = jnp.transpose(vs_ref[rs, :]).astype(BF16)
        vwb_ref[:, rs] = jnp.transpose(vw_ref[rs, :]).astype(BF16)
        return carry

    lax.fori_loop(0, seq // ROPE_ROWS, rope_rows, 0)

    def compress(src_ref, pos_ref, w1_ref, w2_ref, out_ref, transpose_out):
        parts = [jnp.zeros((ncmp, CMP_HIDDEN), F32) for _ in range(per)]
        for j in range(CMP_STRIDE):
            xj = src_ref[pl.ds(j, ncmp, stride=CMP_STRIDE), :]
            for p in range(per):
                jj = p * CMP_STRIDE + j
                xp = (xj + pos_ref[jj:jj + 1, :]).astype(BF16)
                parts[p] = parts[p] + jnp.dot(xp, w1_ref[jj * HEAD_DIM:(jj + 1) * HEAD_DIM, :],
                                              preferred_element_type=F32)
        pre = parts[0]
        for p in range(1, per):
            pre = pre + pltpu.roll(parts[p], ncmp - p, 0)
        hid = _gelu_tanh(pre)
        out = jnp.dot(hid.astype(BF16), w2_ref[...], preferred_element_type=F32)
        out_ref[...] = (jnp.transpose(out) if transpose_out else out).astype(BF16)

    compress(kcr_s, posk_ref, w1k_ref, w2k_ref, kcmp_ref, False)
    compress(vc_ref, posv_ref, w1v_ref, w2v_ref, vcmp_ref, True)


def _nsa_prep(p_nsa, cosf, sins, posk, w1k, w2k, posv, w1v, w2v):
    bsz, seq, _ = p_nsa.shape
    ng = NSA_KV
    ncmp = seq // CMP_STRIDE
    q_blocks = NSA_W // LANES

    def part(k):
        return pl.BlockSpec((None, seq, LANES), lambda b, g, k=k: (b, 0, q_blocks + ng * k + g))

    def whole(shape):
        return pl.BlockSpec(shape, lambda b, g: (0,) * len(shape))

    seq_out = pl.BlockSpec((None, None, seq, LANES), lambda b, g: (b, g, 0, 0))
    seq_out_t = pl.BlockSpec((None, None, LANES, seq), lambda b, g: (b, g, 0, 0))
    cmp_out = pl.BlockSpec((None, None, ncmp, LANES), lambda b, g: (b, g, 0, 0))
    seq_shape = jax.ShapeDtypeStruct((bsz, ng, seq, LANES), BF16)
    seq_shape_t = jax.ShapeDtypeStruct((bsz, ng, LANES, seq), BF16)
    cmp_shape = jax.ShapeDtypeStruct((bsz, ng, ncmp, LANES), BF16)
    return pl.pallas_call(
        functools.partial(_nsa_prep_kernel, seq=seq),
        grid=(bsz, ng),
        in_specs=[part(0), part(1), part(2), part(3), part(4), part(5),
                  whole((seq, LANES)), whole((seq, LANES)),
                  whole((CMP_LEN, HEAD_DIM)), whole((CMP_LEN * HEAD_DIM, CMP_HIDDEN)), whole((CMP_HIDDEN, HEAD_DIM)),
                  whole((CMP_LEN, HEAD_DIM)), whole((CMP_LEN * HEAD_DIM, CMP_HIDDEN)), whole((CMP_HIDDEN, HEAD_DIM))],
        out_specs=[cmp_out, cmp_out, seq_out, seq_out_t, seq_out, seq_out_t],
        out_shape=[cmp_shape, cmp_shape, seq_shape, seq_shape_t, seq_shape, seq_shape_t],
        scratch_shapes=[pltpu.VMEM((seq, LANES), F32)],
        compiler_params=_params(("parallel", "parallel"), VMEM_MIXER_MIB),
        name="nsa_prep",
    )(p_nsa, p_nsa, p_nsa, p_nsa, p_nsa, p_nsa, cosf, sins, posk, w1k, w2k, posv, w1v, w2v)


def _nsa_attn_kernel(q_ref, cos_ref, sin_ref, kcmp_ref, vcmp_ref, ks_ref, vs_ref, kw_ref, vw_ref,
                     sm_ref, o_ref, sel_s, *, seq):
    tq, sq, tk, hpg = NSA_TQ, NSA_SQ, NSA_TK, NSA_HPG
    streams = range(tq // sq)
    cols = hpg * sq
    scale = HEAD_DIM ** -0.5
    g = pl.program_id(1)
    t0 = pl.program_id(2) * tq
    ncmp = seq // CMP_STRIDE
    nslc = seq // SLC_LEN
    kcmp = kcmp_ref[...]
    vcmp_t = vcmp_ref[...]
    heads = [slice(hh * sq, (hh + 1) * sq) for hh in range(hpg)]

    def rope_q_t(st):
        rs = slice(st * sq, (st + 1) * sq)
        cosf = cos_ref[rs, :] * (scale * LOG2E)
        sins = sin_ref[rs, :] * (scale * LOG2E)
        out = []
        for hh in range(hpg):
            x = q_ref[rs, hh * HEAD_DIM:(hh + 1) * HEAD_DIM]
            out.append(jnp.transpose(x * cosf + pltpu.roll(x, HEAD_DIM // 2, 1) * sins).astype(BF16))
        return jnp.concatenate(out, axis=1)

    qts = [rope_q_t(st) for st in streams]
    t_ls = [t0 + st * sq + lax.broadcasted_iota(jnp.int32, (1, sq), 1) for st in streams]

    n_sub = lax.broadcasted_iota(jnp.int32, (ncmp, 1), 0)
    cmasks = [(n_sub * CMP_STRIDE + (CMP_LEN - 1)) <= t_l for t_l in t_ls]
    ss = [jnp.dot(kcmp, qt, preferred_element_type=F32) for qt in qts]
    ss = [[jnp.where(cm, s[:, hd], NEG) for hd in heads] for cm, s in zip(cmasks, ss)]
    es = [[jnp.exp2(s - jnp.max(s, axis=0, keepdims=True)) for s in sh] for sh in ss]
    p_cmps = [[jnp.where(cm, e * (1.0 / jnp.sum(e, axis=0, keepdims=True)), 0.0) for e in eh]
              for cm, eh in zip(cmasks, es)]
    o_cmps = [jnp.dot(vcmp_t, jnp.concatenate(ph, axis=1).astype(BF16), preferred_element_type=F32)
              for ph in p_cmps]

    j_i = lax.broadcasted_iota(jnp.int32, (nslc, ncmp), 0)
    n_i = lax.broadcasted_iota(jnp.int32, (nslc, ncmp), 1)
    ovl = (jnp.minimum(n_i * CMP_STRIDE + CMP_LEN, j_i * SLC_LEN + SLC_LEN)
           - jnp.maximum(n_i * CMP_STRIDE, j_i * SLC_LEN))
    ovl = (jnp.maximum(ovl, 0).astype(F32) * (1.0 / CMP_LEN)).astype(BF16)
    imps = [_dot_exact_lhs(ovl, functools.reduce(lambda a, b: a + b, ph)) for ph in p_cmps]
    blk = lax.broadcasted_iota(jnp.int32, (nslc, 1), 0)
    imps = [jnp.where((blk == 0) | (blk == t_l // SLC_LEN) | (blk == t_l // SLC_LEN - 1), jnp.inf, imp)
            for imp, t_l in zip(imps, t_ls)]
    imps = [jnp.where(blk * SLC_LEN <= t_l, imp, -jnp.inf) for imp, t_l in zip(imps, t_ls)]
    ranks = [jnp.zeros((nslc, sq), F32) for _ in streams]
    for i in range(nslc):
        ranks = [rank + jnp.where((imp[i:i + 1, :] > imp) | ((imp[i:i + 1, :] == imp) & (blk > i)), 1.0, 0.0)
                 for rank, imp in zip(ranks, imps)]
    for st, rank in zip(streams, ranks):
        sel_s[st * nslc:(st + 1) * nslc, :] = jnp.where(rank < float(min(SLC_TOP, nslc)), 1.0, 0.0)

    key_sub = lax.broadcasted_iota(jnp.int32, (tk, 1), 0)
    slc_sub = lax.broadcasted_iota(jnp.int32, (SLC_LEN, 1), 0)

    units = [(st, hd) for st in streams for hd in heads]

    def attend(kb, carries, jobs):
        k0 = pl.multiple_of(kb * tk, tk)
        work = []
        for j, (k_ref, vt_ref, bias_fn) in enumerate(jobs):
            kblk = k_ref[pl.ds(k0, tk), :]
            vblk_t = vt_ref[:, pl.ds(k0, tk)]
            biases = bias_fn(k0)
            work += [(j, u, kblk, vblk_t, biases[st], qts[st][:, hd]) for u, (st, hd) in enumerate(units)]
        carries = [list(c) for c in carries]

        def scores(w):
            return jnp.dot(work[w][2], work[w][5], preferred_element_type=F32)

        s_next = scores(0)
        for w, (j, u, _, vblk_t, bias, _) in enumerate(work):
            sh = s_next + bias
            if w + 1 < len(work):
                s_next = scores(w + 1)
            m_i, l_i, acc = carries[j][u]
            m_new = jnp.maximum(m_i, jnp.max(sh, axis=0, keepdims=True))
            alpha = jnp.exp2(m_i - m_new)
            p = jnp.exp2(sh - m_new)
            l_new = alpha * l_i + jnp.sum(p, axis=0, keepdims=True)
            pv = jnp.dot(vblk_t, p.astype(BF16), preferred_element_type=F32)
            carries[j][u] = (m_new, l_new, alpha * acc + pv)
        return tuple(tuple(c) for c in carries)

    init = tuple((jnp.full((1, sq), NEG, F32), jnp.zeros((1, sq), F32), jnp.zeros((HEAD_DIM, sq), F32))
                 for _ in units)

    def slc_bias(k0):
        out = []
        for st, t_l in zip(streams, t_ls):
            pieces = []
            for i in range(tk // SLC_LEN):
                picked = sel_s[pl.ds(st * nslc + k0 // SLC_LEN + i, 1), :]
                kpos = k0 + i * SLC_LEN + slc_sub
                pieces.append(jnp.where((picked > 0.5) & (kpos <= t_l), 0.0, NEG))
            out.append(jnp.concatenate(pieces, axis=0))
        return out

    def win_bias(k0):
        kpos = k0 + key_sub
        return [jnp.where((kpos <= t_l) & (kpos > t_l - WINDOW), 0.0, NEG) for t_l in t_ls]

    n_blocks = (t0 + tq + tk - 1) // tk
    win_lo = jnp.maximum(t0 - WINDOW + 1, 0) // tk
    slc_job = (ks_ref, vs_ref, slc_bias)
    win_job = (kw_ref, vw_ref, win_bias)
    (slc,) = lax.fori_loop(0, win_lo, lambda kb, c: attend(kb, c, [slc_job]), (init,))
    slc, win = lax.fori_loop(win_lo, n_blocks, lambda kb, c: attend(kb, c, [slc_job, win_job]), (slc, init))
    o_slcs = [acc * (1.0 / l) for _, l, acc in slc]
    o_wins = [acc * (1.0 / l) for _, l, acc in win]

    gate_row = lax.broadcasted_iota(jnp.int32, (SM_W, sq), 0)
    for st in streams:
        rs = slice(st * sq, (st + 1) * sq)
        sm_t = jnp.transpose(sm_ref[rs, :])

        def gate(col):
            return jax.nn.sigmoid(jnp.sum(jnp.where(gate_row == col, sm_t, 0.0), axis=0, keepdims=True))

        for hh, hd in enumerate(heads):
            col = GATE_COL0 + (g * hpg + hh) * 3
            u = st * hpg + hh
            o_t = (gate(col) * o_cmps[st][:, hd] + gate(col + 1) * o_slcs[u]
                   + gate(col + 2) * o_wins[u])
            o_ref[rs, hh * HEAD_DIM:(hh + 1) * HEAD_DIM] = jnp.transpose(o_t).astype(o_ref.dtype)


def _nsa_attn(p_nsa, p_sm, cosf, sins, kcmp, vcmp, ksr, vsb, kwr, vwb):
    bsz, seq, _ = p_nsa.shape
    ng = NSA_KV
    tq = NSA_TQ
    assert seq % tq == 0 and seq // CMP_STRIDE == LANES
    ncmp = seq // CMP_STRIDE
    gw = NSA_HPG * HEAD_DIM
    per_group = lambda rows: pl.BlockSpec((None, None, rows, LANES), lambda b, g, t: (b, g, 0, 0))
    per_group_t = pl.BlockSpec((None, None, LANES, seq), lambda b, g, t: (b, g, 0, 0))
    return pl.pallas_call(
        functools.partial(_nsa_attn_kernel, seq=seq),
        grid=(bsz, ng, seq // tq),
        in_specs=[pl.BlockSpec((None, tq, gw), lambda b, g, t: (b, t, g)),
                  pl.BlockSpec((tq, LANES), lambda b, g, t: (t, 0)),
                  pl.BlockSpec((tq, LANES), lambda b, g, t: (t, 0)),
                  per_group(ncmp), per_group(ncmp),
                  per_group(seq), per_group_t, per_group(seq), per_group_t,
                  pl.BlockSpec((None, tq, SM_W), lambda b, g, t: (b, t, 0))],
        out_specs=pl.BlockSpec((None, tq, gw), lambda b, g, t: (b, t, g)),
        out_shape=jax.ShapeDtypeStruct((bsz, seq, NSA_W), BF16),
        scratch_shapes=[pltpu.VMEM((tq // NSA_SQ * (seq // SLC_LEN), NSA_SQ), F32)],
        compiler_params=_params(("parallel", "parallel", "arbitrary"), VMEM_MIXER_MIB),
        name="nsa_attn",
    )(p_nsa, cosf, sins, kcmp, vcmp, ksr, vsb, kwr, vwb, p_sm)


def _rope_tables(seq):
    inv = ROPE_THETA ** (-jnp.arange(0, HEAD_DIM, 2, dtype=F32) / HEAD_DIM)
    ang = jnp.arange(seq, dtype=F32)[:, None] * inv[None, :]
    cos, sin = jnp.cos(ang), jnp.sin(ang)
    return jnp.concatenate([cos, cos], axis=-1), jnp.concatenate([-sin, sin], axis=-1)


def _mixers(p_hg, p_gd, p_nsa, p_sm, lb, hg_norm, conv_w, a_log, dt_bias, gd_norm,
            posk, w1k, w2k, posv, w1v, w2v, cosf, sins):
    o_a, o_b = _hgrn_gdn(p_hg, lb, hg_norm, p_gd, p_sm, conv_w, a_log, dt_bias, gd_norm)
    prep = _nsa_prep(p_nsa, cosf, sins, posk, w1k.astype(BF16), w2k.astype(BF16),
                     posv, w1v.astype(BF16), w2v.astype(BF16))
    o_c = _nsa_attn(p_nsa, p_sm, cosf, sins, *prep)
    return o_a, o_b, o_c


def kernel(x, attn_norm, w_in, w_out, ffn_norm, w_gate, w_up, w_down, final_norm, hgrn_lb_logits, hgrn_out_norm, gdn_conv, gdn_A_log, gdn_dt_bias, gdn_out_norm, cmp_pos_k, cmp_w1_k, cmp_w2_k, cmp_pos_v, cmp_w1_v, cmp_w2_v):
    bsz, seq, d = x.shape
    depth = w_in.shape[0]
    n = bsz * seq
    cosf, sins = _rope_tables(seq)
    lb_p = jax.nn.softmax(hgrn_lb_logits.astype(F32), axis=0)
    lb_all = jnp.concatenate([jnp.zeros_like(lb_p[:1]), jnp.cumsum(lb_p, axis=0)[:-1]], axis=0)

    c_gd = 4 * HG_W
    c_ab = c_gd + 4 * GD_W
    c_nsa = c_ab + 2 * GD_HEADS
    c_gate = c_nsa + NSA_W + 6 * NSA_KV_W
    n_gate = NSA_HEADS * 3

    w_t = jnp.swapaxes(w_in, 1, 2)
    w_nsa = w_t[:, c_nsa:c_gate, :]
    w_sm = jnp.concatenate(
        [w_t[:, c_ab:c_nsa, :], w_t[:, c_gate:, :],
         jnp.zeros((depth, SM_W - 2 * GD_HEADS - n_gate, d), F32)], axis=1)
    w_dn = w_down.astype(BF16)

    xf = x.reshape(n, d)
    xw, ss = _norm_inputs(xf, attn_norm[0])
    for l in range(depth):
        p_hg = _matmul_nt(xw, ss, w_t, l, 0, c_gd).reshape(bsz, seq, -1)
        p_gd = _matmul_nt(xw, ss, w_t, l, c_gd, c_ab - c_gd).reshape(bsz, seq, -1)
        p_nsa = _matmul_nt(xw, ss, w_nsa, l).reshape(bsz, seq, -1)
        p_sm = _matmul_nt(xw, ss, w_sm, l).reshape(bsz, seq, -1)
        mix = _mixers(p_hg, p_gd, p_nsa, p_sm, lb_all[l], hgrn_out_norm[l], gdn_conv[l],
                      gdn_A_log[l], gdn_dt_bias[l], gdn_out_norm[l],
                      cmp_pos_k[l], cmp_w1_k[l], cmp_w2_k[l], cmp_pos_v[l], cmp_w1_v[l], cmp_w2_v[l],
                      cosf, sins)
        xf, xw, ss = _out_proj([o.reshape(n, o.shape[-1]) for o in mix], w_out, l, xf, ffn_norm[l])
        act = _gate_up(xw, ss, w_gate, w_up, l)
        next_norm = attn_norm[l + 1] if l + 1 < depth else final_norm
        xf, xw, ss = _matmul_ktiled_res(act, w_dn, l, xf, next_norm)
    return _rmsnorm(xf, final_norm, F32).reshape(bsz, seq, d)
```

```python
import functools

import jax
import jax.numpy as jnp
from jax import lax
from jax.experimental import pallas as pl
from jax.experimental.pallas import tpu as pltpu

F32 = jnp.float32
BF16 = jnp.bfloat16

HEAD_DIM = 128
HG_HEADS = 8
GD_HEADS = 8
CONV_W = 4
NSA_HEADS = 16
NSA_KV = 4
NSA_HPG = NSA_HEADS // NSA_KV
CMP_LEN = 32
CMP_STRIDE = 16
CMP_HIDDEN = 256
SLC_LEN = 64
SLC_TOP = 16
WINDOW = 512
ROPE_THETA = 10000.0
NORM_EPS = 1e-6
LB_FLOOR = 1e-30
NEG = -1e30
LOG2E = 1.4426950408889634
FFN_HIDDEN = 11008

HG_W = HG_HEADS * HEAD_DIM
GD_W = GD_HEADS * HEAD_DIM
NSA_W = NSA_HEADS * HEAD_DIM
NSA_KV_W = NSA_KV * HEAD_DIM
SM_W = 128
GATE_COL0 = 2 * GD_HEADS

LANES = 128
MIB = 1 << 20

MM_TM, MM_TN = 1024, 512
FFN_TN = 256
DOWN_TK = 5504
NORM_ROWS = 256
VMEM_MATMUL_MIB = 56
VMEM_DOWN_MIB = 48
VMEM_ROWWISE_MIB = 40
VMEM_MIXER_MIB = 48

NT = (((1,), (1,)), ((), ()))
TN = (((0,), (0,)), ((), ()))


def _params(sem, vmem_mib):
    return pltpu.CompilerParams(dimension_semantics=sem, vmem_limit_bytes=vmem_mib * MIB)


def _split2(x):
    hi = x.astype(BF16)
    return hi, (x - hi.astype(F32)).astype(BF16)


def _dot_x3(a, b):
    ah, al = _split2(a)
    bh, bl = _split2(b)
    d = lambda x, y: jnp.dot(x, y, preferred_element_type=F32)
    return d(ah, bh) + (d(ah, bl) + d(al, bh))


def _dot_exact_lhs(a, b):
    hi = b.astype(BF16)
    r = b - hi.astype(F32)
    mid = r.astype(BF16)
    lo = (r - mid.astype(F32)).astype(BF16)
    d = lambda y: jnp.dot(a, y, preferred_element_type=F32)
    return d(hi) + (d(mid) + d(lo))


def _silu(x):
    return x * jax.nn.sigmoid(x)


def _softplus(x):
    return jnp.maximum(x, 0.0) + jnp.log1p(jnp.exp(-jnp.abs(x)))


def _lane_pick(x, lane_ids, idx):
    return jnp.sum(jnp.where(lane_ids == idx, x, 0.0), axis=-1, keepdims=True)


def _rms_kernel(x_ref, w_ref, o_ref):
    x = x_ref[...]
    ms = jnp.mean(x * x, axis=-1, keepdims=True)
    o_ref[...] = (x * lax.rsqrt(ms + NORM_EPS) * w_ref[...]).astype(o_ref.dtype)


def _rmsnorm(x, w, out_dtype, rows=NORM_ROWS):
    n, d = x.shape
    return pl.pallas_call(
        _rms_kernel,
        grid=(n // rows,),
        in_specs=[pl.BlockSpec((rows, d), lambda i: (i, 0)),
                  pl.BlockSpec((1, d), lambda i: (0, 0))],
        out_specs=pl.BlockSpec((rows, d), lambda i: (i, 0)),
        out_shape=jax.ShapeDtypeStruct((n, d), out_dtype),
        compiler_params=_params(("parallel",), VMEM_ROWWISE_MIB),
        name="rmsnorm",
    )(x, w.reshape(1, d))


def _rinv(ss_ref, d):
    return lax.rsqrt(ss_ref[...] * (1.0 / d) + NORM_EPS)


def _emit_norm_inputs(x, nw_ref, xw_ref, ss_ref, first):
    xw_ref[...] = (x * nw_ref[...]).astype(xw_ref.dtype)
    part = jnp.sum(x * x, axis=-1, keepdims=True)

    @pl.when(first)
    def _():
        ss_ref[...] = part

    @pl.when(jnp.logical_not(first))
    def _():
        ss_ref[...] += part


def _norm_inputs_kernel(x_ref, w_ref, xw_ref, ss_ref):
    x = x_ref[...]
    xw_ref[...] = (x * w_ref[...]).astype(xw_ref.dtype)
    ss_ref[...] = jnp.sum(x * x, axis=-1, keepdims=True)


def _norm_inputs(x, w, rows=NORM_ROWS):
    n, d = x.shape
    return pl.pallas_call(
        _norm_inputs_kernel,
        grid=(n // rows,),
        in_specs=[pl.BlockSpec((rows, d), lambda i: (i, 0)),
                  pl.BlockSpec((1, d), lambda i: (0, 0))],
        out_specs=[pl.BlockSpec((rows, d), lambda i: (i, 0)), pl.BlockSpec((rows, 1), lambda i: (i, 0))],
        out_shape=[jax.ShapeDtypeStruct((n, d), BF16), jax.ShapeDtypeStruct((n, 1), F32)],
        compiler_params=_params(("parallel",), VMEM_ROWWISE_MIB),
        name="norm_inputs",
    )(x, w.reshape(1, d))


def _mm_nt_kernel(a_ref, b_ref, ss_ref, o_ref):
    acc = lax.dot_general(a_ref[...], b_ref[...].astype(BF16), NT, preferred_element_type=F32)
    o_ref[...] = _rinv(ss_ref, a_ref.shape[1]) * acc


def _matmul_nt(xw, ss, w_t, layer, row0=0, n=None, tm=MM_TM, tn=MM_TN):
    m, k = xw.shape
    n = w_t.shape[1] if n is None else n
    tn = min(tn, n)
    assert m % tm == 0 and n % tn == 0 and row0 % tn == 0
    rb = row0 // tn
    return pl.pallas_call(
        _mm_nt_kernel,
        grid=(m // tm, n // tn),
        in_specs=[pl.BlockSpec((tm, k), lambda i, j: (i, 0)),
                  pl.BlockSpec((None, tn, k), lambda i, j: (layer, rb + j, 0)),
                  pl.BlockSpec((tm, 1), lambda i, j: (i, 0))],
        out_specs=pl.BlockSpec((tm, tn), lambda i, j: (i, j)),
        out_shape=jax.ShapeDtypeStruct((m, n), F32),
        compiler_params=_params(("parallel", "parallel"), VMEM_MATMUL_MIB),
        name="matmul",
    )(xw, w_t, ss)


def _residual_outputs(m, n, tm, tn, index):
    specs = [pl.BlockSpec((tm, tn), index), pl.BlockSpec((tm, tn), index),
             pl.BlockSpec((tm, 1), lambda i, *_: (i, 0))]
    shapes = [jax.ShapeDtypeStruct((m, n), F32), jax.ShapeDtypeStruct((m, n), BF16),
              jax.ShapeDtypeStruct((m, 1), F32)]
    return specs, shapes


def _out_proj_kernel(a_ref, b_ref, c_ref, w_ref, r_ref, nw_ref, o_ref, xw_ref, ss_ref):
    ka, kb = a_ref.shape[1], b_ref.shape[1]
    acc = r_ref[...] + jnp.dot(a_ref[...], w_ref[0:ka, :].astype(BF16), preferred_element_type=F32)
    acc = acc + jnp.dot(b_ref[...], w_ref[ka:ka + kb, :].astype(BF16), preferred_element_type=F32)
    x = acc + jnp.dot(c_ref[...], w_ref[ka + kb:, :].astype(BF16), preferred_element_type=F32)
    o_ref[...] = x
    _emit_norm_inputs(x, nw_ref, xw_ref, ss_ref, pl.program_id(1) == 0)


def _out_proj(parts, w, layer, res, next_norm_w, tm=MM_TM, tn=MM_TN):
    m, n = res.shape
    k = w.shape[1]
    tile = lambda i, j: (i, j)
    out_specs, out_shape = _residual_outputs(m, n, tm, tn, tile)
    return pl.pallas_call(
        _out_proj_kernel,
        grid=(m // tm, n // tn),
        in_specs=[pl.BlockSpec((tm, p.shape[1]), lambda i, j: (i, 0)) for p in parts]
        + [pl.BlockSpec((None, k, tn), lambda i, j: (layer, 0, j)),
           pl.BlockSpec((tm, tn), tile), pl.BlockSpec((1, tn), lambda i, j: (0, j))],
        out_specs=out_specs,
        out_shape=out_shape,
        compiler_params=_params(("parallel", "arbitrary"), VMEM_MATMUL_MIB),
        name="out_proj",
    )(*parts, w, res, next_norm_w.reshape(1, n))


def _mm_acc_res_kernel(a_ref, b_ref, r_ref, nw_ref, o_ref, xw_ref, ss_ref, acc_ref, *, nk):
    k = pl.program_id(2)

    @pl.when(k == 0)
    def _():
        acc_ref[...] = r_ref[...]

    acc_ref[...] += jnp.dot(a_ref[...], b_ref[...], preferred_element_type=F32)

    @pl.when(k == nk - 1)
    def _():
        x = acc_ref[...]
        o_ref[...] = x
        _emit_norm_inputs(x, nw_ref, xw_ref, ss_ref, pl.program_id(1) == 0)


def _matmul_ktiled_res(a, b, layer, res, next_norm_w, tm=MM_TM, tn=MM_TN, tk=DOWN_TK):
    m, k = a.shape
    n = b.shape[2]
    nk = k // tk
    tile = lambda i, j, kk: (i, j)
    out_specs, out_shape = _residual_outputs(m, n, tm, tn, tile)
    return pl.pallas_call(
        functools.partial(_mm_acc_res_kernel, nk=nk),
        grid=(m // tm, n // tn, nk),
        in_specs=[pl.BlockSpec((tm, tk), lambda i, j, kk: (i, kk)),
                  pl.BlockSpec((None, tk, tn), lambda i, j, kk: (layer, kk, j)),
                  pl.BlockSpec((tm, tn), tile), pl.BlockSpec((1, tn), lambda i, j, kk: (0, j))],
        out_specs=out_specs,
        out_shape=out_shape,
        scratch_shapes=[pltpu.VMEM((tm, tn), F32)],
        compiler_params=_params(("parallel", "arbitrary", "arbitrary"), VMEM_DOWN_MIB),
        name="matmul_down",
    )(a, b, res, next_norm_w.reshape(1, n))


def _gate_up_kernel(a_ref, wg_ref, wu_ref, ss_ref, o_ref):
    a = a_ref[...]
    rinv = _rinv(ss_ref, a_ref.shape[1])
    g = rinv * jnp.dot(a, wg_ref[...].astype(BF16), preferred_element_type=F32)
    u = rinv * jnp.dot(a, wu_ref[...].astype(BF16), preferred_element_type=F32)
    o_ref[...] = (_silu(g) * u).astype(o_ref.dtype)


def _gate_up(xw, ss, wg, wu, layer, tm=MM_TM, tn=FFN_TN):
    m, k = xw.shape
    n = wg.shape[2]
    return pl.pallas_call(
        _gate_up_kernel,
        grid=(m // tm, n // tn),
        in_specs=[pl.BlockSpec((tm, k), lambda i, j: (i, 0)),
                  pl.BlockSpec((None, k, tn), lambda i, j: (layer, 0, j)),
                  pl.BlockSpec((None, k, tn), lambda i, j: (layer, 0, j)),
                  pl.BlockSpec((tm, 1), lambda i, j: (i, 0))],
        out_specs=pl.BlockSpec((tm, tn), lambda i, j: (i, j)),
        out_shape=jax.ShapeDtypeStruct((m, n), BF16),
        compiler_params=_params(("parallel", "parallel"), VMEM_MATMUL_MIB),
        name="ffn_gate_up",
    )(xw, wg, wu, ss)


HG_ROWS = 64
HG_SUB = 16
HG_GROUP = 2


def _hgrn_phases(q_ref, f_ref, i_ref, g_ref, lb_ref, nw_ref, o_ref):
    rows, sub = HG_ROWS, HG_SUB
    row_id = lax.broadcasted_iota(jnp.int32, (rows, LANES), 0)
    sub_id = row_id & (sub - 1)
    r2 = lax.broadcasted_iota(jnp.int32, (rows, rows), 0)
    c2 = lax.broadcasted_iota(jnp.int32, (rows, rows), 1)
    tri = jnp.where(((r2 // sub) == (c2 // sub)) & (c2 <= r2), 1.0, 0.0).astype(BF16)
    log_lb = jnp.log(jnp.maximum(lb_ref[...], LB_FLOOR))
    nw = nw_ref[...]
    nsub = rows // sub

    def features(c):
        r0 = pl.multiple_of(c * rows, rows)
        q = _silu(q_ref[pl.ds(r0, rows), :])
        fp = f_ref[pl.ds(r0, rows), :]
        v = i_ref[pl.ds(r0, rows), :]
        logf = jnp.minimum(-_softplus(-fp) + _softplus(log_lb - fp), 0.0)
        return q, 1.0 - jnp.exp(logf), v, logf

    def phases(i, box):
        cs = [i * HG_GROUP + j for j in range(HG_GROUP)]
        d_ = {}

        def load():
            d_["feats"] = [features(c) for c in cs]

        def cumsum():
            d_["bs"] = [_dot_exact_lhs(tri, f[3]) * LOG2E for f in d_["feats"]]

        def intra():
            feats, bs = d_["feats"], d_["bs"]
            accs = [jnp.sum(q * kf, axis=-1, keepdims=True) * v for q, kf, v, _ in feats]
            for d in range(1, sub):
                prods = [jnp.where(sub_id >= d, q * pltpu.roll(kf, d, 0)
                                   * jnp.exp2(jnp.minimum(b - pltpu.roll(b, d, 0), 0.0)), 0.0)
                         for (q, kf, _, _), b in zip(feats, bs)]
                accs = [acc + jnp.sum(p, axis=-1, keepdims=True) * pltpu.roll(f[2], d, 0)
                        for acc, p, f in zip(accs, prods, feats)]
            d_["accs"] = accs

        def products():
            pieces = []
            for (q, kf, v, _), b in zip(d_["feats"], d_["bs"]):
                for j in range(nsub):
                    sl = slice(j * sub, (j + 1) * sub)
                    bj = b[sl]
                    bl = bj[sub - 1:sub]
                    kh = (kf[sl] * jnp.exp2(bl - bj)).astype(BF16)
                    vk = lax.dot_general(v[sl].astype(BF16), kh, TN, preferred_element_type=F32)
                    pieces.append(((q[sl] * jnp.exp2(bj)).astype(BF16), jnp.exp2(bl), vk))
            d_["pieces"] = pieces

        def chain():
            state, inter = box["s"], []
            for qt, decay, vk in d_["pieces"]:
                inter.append(lax.dot_general(qt, state.astype(BF16), NT, preferred_element_type=F32))
                state = decay * state + vk
            box["s"], d_["inter"] = state, inter

        def emit():
            for n, c in enumerate(cs):
                r0 = pl.multiple_of(c * rows, rows)
                o = d_["accs"][n] + jnp.concatenate(d_["inter"][n * nsub:(n + 1) * nsub], axis=0)
                ms = jnp.mean(o * o, axis=-1, keepdims=True)
                y = o * lax.rsqrt(ms + NORM_EPS) * nw
                y = y * _silu(g_ref[pl.ds(r0, rows), :])
                o_ref[pl.ds(r0, rows), :] = y.astype(o_ref.dtype)

        return [load, cumsum, intra, products, chain, emit]

    return phases


GD_CHUNK = 64
GD_GROUP = 8


def _gdn_program(q_ref, k_ref, v_ref, z_ref, sm_ref, cq_ref, ck_ref, cv_ref, al_ref, dtb_ref, nw_ref,
                 o_ref, am_s, bm_s, qp_s, op_s, dl_s):
    ch = GD_CHUNK
    h = pl.program_id(1)
    lane1 = lax.broadcasted_iota(jnp.int32, (1, LANES), 1)
    lane = lax.broadcasted_iota(jnp.int32, (ch, LANES), 1)
    neg_a = -jnp.exp(_lane_pick(al_ref[...], lane1, h))
    dt_bias = _lane_pick(dtb_ref[...], lane1, h)
    r2 = lax.broadcasted_iota(jnp.int32, (ch, ch), 0)
    c2 = lax.broadcasted_iota(jnp.int32, (ch, ch), 1)
    incl = c2 <= r2
    strict = c2 < r2
    tri = jnp.where(incl, 1.0, 0.0).astype(BF16)
    eye = jnp.where(c2 == r2, 1.0, 0.0).astype(F32)
    nw = nw_ref[...]

    def conv_silu(ref, w_ref, r0, c):
        cur = ref[pl.ds(r0, ch), :]
        prev = ref[pl.ds(jnp.maximum(r0 - 8, 0), 8), :]
        prev = jnp.where(c > 0, prev, 0.0)
        x = jnp.concatenate([prev, cur], axis=0)
        w = w_ref[...]
        y = cur * w[CONV_W - 1:CONV_W]
        for s in range(1, CONV_W):
            y = y + pltpu.roll(x, s, 0)[8:] * w[CONV_W - 1 - s:CONV_W - s]
        return _silu(y)

    def features(c):
        r0 = pl.multiple_of(c * ch, ch)
        qf = conv_silu(q_ref, cq_ref, r0, c)
        kf = conv_silu(k_ref, ck_ref, r0, c)
        vf = conv_silu(v_ref, cv_ref, r0, c)
        qf = qf * lax.rsqrt(jnp.sum(qf * qf, axis=-1, keepdims=True) + NORM_EPS) * (HEAD_DIM ** -0.5)
        kf = kf * lax.rsqrt(jnp.sum(kf * kf, axis=-1, keepdims=True) + NORM_EPS)
        sm = sm_ref[pl.ds(r0, ch), :]
        a = _lane_pick(sm, lane, h)
        beta = jax.nn.sigmoid(_lane_pick(sm, lane, h + GD_HEADS))
        g = neg_a * _softplus(a + dt_bias)
        return qf, kf, vf, beta, jnp.broadcast_to(g, (ch, LANES))

    def stages(i):
        cs = [i * GD_GROUP + j for j in range(GD_GROUP)]
        d_ = {}

        def load():
            d_["feats"] = [features(c) for c in cs]

        def cumsum():
            d_["gcs"] = [_dot_exact_lhs(tri, f[4]) for f in d_["feats"]]

        def gram():
            feats = d_["feats"]
            d_["decays"] = [jnp.where(incl, jnp.exp(jnp.minimum(gc[:, :ch] - jnp.transpose(gc)[:ch, :], 0.0)), 0.0)
                            for gc in d_["gcs"]]
            kbs = [f[1].astype(BF16) for f in feats]
            kks = [lax.dot_general(kb, kb, NT, preferred_element_type=F32) for kb in kbs]
            d_["qks"] = [lax.dot_general(f[0].astype(BF16), kb, NT, preferred_element_type=F32)
                         for f, kb in zip(feats, kbs)]
            d_["pws"] = [jnp.where(strict, -(f[3] * kk * d), 0.0) for f, kk, d in zip(feats, kks, d_["decays"])]
            d_["invs"] = [eye + pw for pw in d_["pws"]]

        def square():
            d_["pws"] = [_dot_x3(pw, pw) for pw in d_["pws"]]
            d_["invs"] = [inv + _dot_x3(inv, pw) for inv, pw in zip(d_["invs"], d_["pws"])]

        def solve():
            d_["egcs"] = [jnp.exp(gc) for gc in d_["gcs"]]
            d_["wus"] = [_dot_x3(inv, jnp.concatenate([f[1] * (f[3] * egc), f[2] * f[3]], axis=1)).astype(BF16)
                         for inv, f, egc in zip(d_["invs"], d_["feats"], d_["egcs"])]

        def maps():
            d_["qos"] = [jnp.dot(jnp.where(incl, qk * d, 0.0).astype(BF16), wu, preferred_element_type=F32)
                         for qk, d, wu in zip(d_["qks"], d_["decays"], d_["wus"])]
            d_["abs"] = [lax.dot_general((f[1] * jnp.exp(gc[ch - 1:ch] - gc)).astype(BF16), wu, TN,
                                         preferred_element_type=F32)
                         for f, gc, wu in zip(d_["feats"], d_["gcs"], d_["wus"])]

        def store():
            for c, f, gc, egc, qo, ab in zip(cs, d_["feats"], d_["gcs"], d_["egcs"], d_["qos"], d_["abs"]):
                r0 = pl.multiple_of(c * ch, ch)
                m0 = pl.multiple_of(c * LANES, LANES)
                am_s[pl.ds(m0, LANES), :] = ab[:, :LANES].astype(BF16)
                bm_s[pl.ds(m0, LANES), :] = ab[:, LANES:]
                qp_s[pl.ds(r0, ch), :] = (f[0] * egc - qo[:, :LANES]).astype(BF16)
                op_s[pl.ds(r0, ch), :] = qo[:, LANES:]
                dl_s[pl.ds(pl.multiple_of(c * 8, 8), 8), :] = jnp.broadcast_to(jnp.exp(gc[ch - 1:ch]), (8, LANES))

        return [load, cumsum, gram] + [square] * 5 + [solve, maps, store]

    def recur(c, state):
        r0 = pl.multiple_of(c * ch, ch)
        m0 = pl.multiple_of(c * LANES, LANES)
        sb = state.astype(BF16)
        o = jnp.dot(qp_s[pl.ds(r0, ch), :], sb, preferred_element_type=F32) + op_s[pl.ds(r0, ch), :]
        dl = dl_s[pl.ds(pl.multiple_of(c * 8, 8), 1), :]
        state = (dl * state - jnp.dot(am_s[pl.ds(m0, LANES), :], sb, preferred_element_type=F32)
                 + bm_s[pl.ds(m0, LANES), :])
        ms = jnp.mean(o * o, axis=-1, keepdims=True)
        y = o * lax.rsqrt(ms + NORM_EPS) * nw
        y = y * _silu(z_ref[pl.ds(r0, ch), :])
        o_ref[pl.ds(r0, ch), :] = y.astype(o_ref.dtype)
        return state

    return stages, recur


_FUSED_ORDER = ("g0", (0, 0), "g1", (0, 1), "g2", (0, 2), "g3", (0, 3), (0, 4), (0, 5), (1, 0), (1, 1),
                "g4", (1, 2), "g5", (1, 3), (1, 4), (1, 5), (2, 0), (2, 1), "g6", (2, 2), "g7",
                (2, 3), (2, 4), (2, 5), (3, 0), (3, 1), "g8", (3, 2), "g9", (3, 3), (3, 4), (3, 5), "g10")


def _hgrn_gdn_kernel(*refs, seq):
    hg_refs, gd_refs = refs[:6] + refs[18:19], refs[6:17] + refs[19:20] + refs[21:]
    wd_in, wd_out = refs[17], refs[20]
    phases = _hgrn_phases(*hg_refs)
    stages, recur = _gdn_program(*gd_refs)
    nchunks = seq // GD_CHUNK
    steps = GD_GROUP * GD_CHUNK // (HG_ROWS * HG_GROUP)
    wcols = wd_in.shape[1] // (nchunks // GD_GROUP)

    ngroups = nchunks // GD_GROUP
    every = len(_FUSED_ORDER) // GD_GROUP

    def body(i, carry, recur_previous):
        box = {"s": carry[0]}
        gstate = carry[1]
        cols = pl.ds(pl.multiple_of(i * wcols, wcols), wcols)
        wd_out[:, cols] = wd_in[:, cols].astype(wd_out.dtype)
        gd = stages(i)
        hg = [phases(i * steps + k, box) for k in range(steps)]
        for n, item in enumerate(_FUSED_ORDER):
            if isinstance(item, str):
                gd[int(item[1:])]()
            else:
                hg[item[0]][item[1]]()
            if recur_previous and n % every == every - 1 and n // every < GD_GROUP:
                gstate = recur((i - 1) * GD_GROUP + n // every, gstate)
        return box["s"], gstate

    zero = jnp.zeros((LANES, LANES), F32)
    carry = body(jnp.int32(0), (zero, zero), False)
    carry = lax.fori_loop(1, ngroups, lambda i, c: body(i, c, True), carry)
    lax.fori_loop(nchunks - GD_GROUP, nchunks, recur, carry[1], unroll=2)


def _hgrn_gdn(p_hg, lb, hg_norm, p_gd, p_sm, conv_w, a_log, dt_bias, gd_norm, w_cast, layer):
    bsz, seq, _ = p_hg.shape
    nh = HG_HEADS
    nchunks = seq // GD_CHUNK
    assert HG_HEADS == GD_HEADS and seq % (GD_CHUNK * GD_GROUP) == 0
    assert (GD_GROUP * GD_CHUNK) % (HG_ROWS * HG_GROUP) == 0 and len(_FUSED_ORDER) == 11 + 6 * 4
    wk, wn = w_cast.shape[1:]
    wrows, wcols, col_blocks = wk // (2 * bsz), 2 * wn // nh, nh // 2
    assert wk % (2 * bsz) == 0 and wrows % 16 == 0 and wcols % (LANES * (nchunks // GD_GROUP)) == 0
    w_block = lambda b, h: (b * 2 + h // col_blocks, h % col_blocks)

    def part(k):
        return pl.BlockSpec((None, seq, LANES), lambda b, h, k=k: (b, 0, h + nh * k))

    def cpart(k):
        return pl.BlockSpec((CONV_W, LANES), lambda b, h, k=k: (0, h + nh * k))

    pad = lambda t: jnp.pad(t.astype(F32), (0, LANES - nh)).reshape(1, LANES)
    small = pl.BlockSpec((1, LANES), lambda b, h: (0, 0))
    head_out = pl.BlockSpec((None, seq, LANES), lambda b, h: (b, 0, h))
    return pl.pallas_call(
        functools.partial(_hgrn_gdn_kernel, seq=seq),
        grid=(bsz, nh),
        in_specs=[part(0), part(1), part(2), part(3), pl.BlockSpec((1, LANES), lambda b, h: (0, h)), small,
                  part(0), part(1), part(2), part(3),
                  pl.BlockSpec((None, seq, SM_W), lambda b, h: (b, 0, 0)),
                  cpart(0), cpart(1), cpart(2), small, small, small,
                  pl.BlockSpec((None, wrows, wcols), lambda b, h: (layer,) + w_block(b, h))],
        out_specs=[head_out, head_out, pl.BlockSpec((wrows, wcols), w_block)],
        out_shape=[jax.ShapeDtypeStruct((bsz, seq, HG_W), BF16), jax.ShapeDtypeStruct((bsz, seq, GD_W), BF16),
                   jax.ShapeDtypeStruct((wk, wn), BF16)],
        scratch_shapes=[pltpu.VMEM((nchunks * LANES, LANES), BF16), pltpu.VMEM((nchunks * LANES, LANES), F32),
                        pltpu.VMEM((seq, LANES), BF16), pltpu.VMEM((seq, LANES), F32),
                        pltpu.VMEM((nchunks * 8, LANES), F32)],
        compiler_params=_params(("parallel", "parallel"), VMEM_MATMUL_MIB),
        name="hgrn_gdn",
    )(p_hg, p_hg, p_hg, p_hg, lb.reshape(1, HG_W), hg_norm.reshape(1, LANES),
      p_gd, p_gd, p_gd, p_gd, p_sm, conv_w, conv_w, conv_w, pad(a_log), pad(dt_bias),
      gd_norm.reshape(1, LANES), w_cast)


NSA_TQ = 256
NSA_SQ = 128
NSA_TK = 256
ROPE_ROWS = 256


def _gelu_tanh(x):
    return 0.5 * x * (1.0 + jnp.tanh(0.7978845608028654 * (x + 0.044715 * x * x * x)))


def _nsa_prep_kernel(kc_ref, vc_ref, ks_ref, vs_ref, kw_ref, vw_ref, cos_ref, sin_ref,
                     posk_ref, w1k_ref, w2k_ref, posv_ref, w1v_ref, w2v_ref,
                     kcmp_ref, vcmp_ref, ksr_ref, vsb_ref, kwr_ref, vwb_ref, kcr_s, *, seq):
    ncmp = seq // CMP_STRIDE
    per = CMP_LEN // CMP_STRIDE

    def rope_rows(i, carry):
        r0 = pl.multiple_of(i * ROPE_ROWS, ROPE_ROWS)
        rs = pl.ds(r0, ROPE_ROWS)
        cosf = cos_ref[rs, :]
        sins = sin_ref[rs, :]

        def rope(x):
            return x * cosf + pltpu.roll(x, HEAD_DIM // 2, 1) * sins

        kcr_s[rs, :] = rope(kc_ref[rs, :])
        ksr_ref[rs, :] = rope(ks_ref[rs, :]).astype(BF16)
        kwr_ref[rs, :] = rope(kw_ref[rs, :]).astype(BF16)
        vsb_ref[:, rs] = jnp.transpose(vs_ref[rs, :]).astype(BF16)
        vwb_ref[:, rs] = jnp.transpose(vw_ref[rs, :]).astype(BF16)
        return carry

    lax.fori_loop(0, seq // ROPE_ROWS, rope_rows, 0)

    def compress(src_ref, pos_ref, w1_ref, w2_ref, out_ref, transpose_out):
        parts = [jnp.zeros((ncmp, CMP_HIDDEN), F32) for _ in range(per)]
        for j in range(CMP_STRIDE):
            xj = src_ref[pl.ds(j, ncmp, stride=CMP_STRIDE), :]
            for p in range(per):
                jj = p * CMP_STRIDE + j
                xp = (xj + pos_ref[jj:jj + 1, :]).astype(BF16)
                parts[p] = parts[p] + jnp.dot(xp, w1_ref[jj * HEAD_DIM:(jj + 1) * HEAD_DIM, :],
                                              preferred_element_type=F32)
        pre = parts[0]
        for p in range(1, per):
            pre = pre + pltpu.roll(parts[p], ncmp - p, 0)
        hid = _gelu_tanh(pre)
        out = jnp.dot(hid.astype(BF16), w2_ref[...], preferred_element_type=F32)
        out_ref[...] = (jnp.transpose(out) if transpose_out else out).astype(BF16)

    compress(kcr_s, posk_ref, w1k_ref, w2k_ref, kcmp_ref, False)
    compress(vc_ref, posv_ref, w1v_ref, w2v_ref, vcmp_ref, True)


def _nsa_prep(p_nsa, cosf, sins, posk, w1k, w2k, posv, w1v, w2v):
    bsz, seq, _ = p_nsa.shape
    ng = NSA_KV
    ncmp = seq // CMP_STRIDE
    q_blocks = NSA_W // LANES

    def part(k):
        return pl.BlockSpec((None, seq, LANES), lambda b, g, k=k: (b, 0, q_blocks + ng * k + g))

    def whole(shape):
        return pl.BlockSpec(shape, lambda b, g: (0,) * len(shape))

    seq_out = pl.BlockSpec((None, None, seq, LANES), lambda b, g: (b, g, 0, 0))
    seq_out_t = pl.BlockSpec((None, None, LANES, seq), lambda b, g: (b, g, 0, 0))
    cmp_out = pl.BlockSpec((None, None, ncmp, LANES), lambda b, g: (b, g, 0, 0))
    seq_shape = jax.ShapeDtypeStruct((bsz, ng, seq, LANES), BF16)
    seq_shape_t = jax.ShapeDtypeStruct((bsz, ng, LANES, seq), BF16)
    cmp_shape = jax.ShapeDtypeStruct((bsz, ng, ncmp, LANES), BF16)
    return pl.pallas_call(
        functools.partial(_nsa_prep_kernel, seq=seq),
        grid=(bsz, ng),
        in_specs=[part(0), part(1), part(2), part(3), part(4), part(5),
                  whole((seq, LANES)), whole((seq, LANES)),
                  whole((CMP_LEN, HEAD_DIM)), whole((CMP_LEN * HEAD_DIM, CMP_HIDDEN)), whole((CMP_HIDDEN, HEAD_DIM)),
                  whole((CMP_LEN, HEAD_DIM)), whole((CMP_LEN * HEAD_DIM, CMP_HIDDEN)), whole((CMP_HIDDEN, HEAD_DIM))],
        out_specs=[cmp_out, cmp_out, seq_out, seq_out_t, seq_out, seq_out_t],
        out_shape=[cmp_shape, cmp_shape, seq_shape, seq_shape_t, seq_shape, seq_shape_t],
        scratch_shapes=[pltpu.VMEM((seq, LANES), F32)],
        compiler_params=_params(("parallel", "parallel"), VMEM_MIXER_MIB),
        name="nsa_prep",
    )(p_nsa, p_nsa, p_nsa, p_nsa, p_nsa, p_nsa, cosf, sins, posk, w1k, w2k, posv, w1v, w2v)


def _nsa_attn_kernel(q_ref, cos_ref, sin_ref, kcmp_ref, vcmp_ref, ks_ref, vs_ref, kw_ref, vw_ref,
                     sm_ref, o_ref, sel_s, *, seq):
    tq, sq, tk, hpg = NSA_TQ, NSA_SQ, NSA_TK, NSA_HPG
    streams = range(tq // sq)
    cols = hpg * sq
    scale = HEAD_DIM ** -0.5
    g = pl.program_id(1)
    t0 = pl.program_id(2) * tq
    ncmp = seq // CMP_STRIDE
    nslc = seq // SLC_LEN
    kcmp = kcmp_ref[...]
    vcmp_t = vcmp_ref[...]
    heads = [slice(hh * sq, (hh + 1) * sq) for hh in range(hpg)]

    def rope_q_t(st):
        rs = slice(st * sq, (st + 1) * sq)
        cosf = cos_ref[rs, :] * (scale * LOG2E)
        sins = sin_ref[rs, :] * (scale * LOG2E)
        out = []
        for hh in range(hpg):
            x = q_ref[rs, hh * HEAD_DIM:(hh + 1) * HEAD_DIM]
            out.append(jnp.transpose(x * cosf + pltpu.roll(x, HEAD_DIM // 2, 1) * sins).astype(BF16))
        return jnp.concatenate(out, axis=1)

    qts = [rope_q_t(st) for st in streams]
    t_ls = [t0 + st * sq + lax.broadcasted_iota(jnp.int32, (1, sq), 1) for st in streams]

    n_sub = lax.broadcasted_iota(jnp.int32, (ncmp, 1), 0)
    cmasks = [(n_sub * CMP_STRIDE + (CMP_LEN - 1)) <= t_l for t_l in t_ls]
    ss = [jnp.dot(kcmp, qt, preferred_element_type=F32) for qt in qts]
    ss = [[jnp.where(cm, s[:, hd], NEG) for hd in heads] for cm, s in zip(cmasks, ss)]
    es = [[jnp.exp2(s - jnp.max(s, axis=0, keepdims=True)) for s in sh] for sh in ss]
    p_cmps = [[jnp.where(cm, e * (1.0 / jnp.sum(e, axis=0, keepdims=True)), 0.0) for e in eh]
              for cm, eh in zip(cmasks, es)]
    o_cmps = [jnp.dot(vcmp_t, jnp.concatenate(ph, axis=1).astype(BF16), preferred_element_type=F32)
              for ph in p_cmps]

    j_i = lax.broadcasted_iota(jnp.int32, (nslc, ncmp), 0)
    n_i = lax.broadcasted_iota(jnp.int32, (nslc, ncmp), 1)
    ovl = (jnp.minimum(n_i * CMP_STRIDE + CMP_LEN, j_i * SLC_LEN + SLC_LEN)
           - jnp.maximum(n_i * CMP_STRIDE, j_i * SLC_LEN))
    ovl = (jnp.maximum(ovl, 0).astype(F32) * (1.0 / CMP_LEN)).astype(BF16)
    imps = [_dot_exact_lhs(ovl, functools.reduce(lambda a, b: a + b, ph)) for ph in p_cmps]
    blk = lax.broadcasted_iota(jnp.int32, (nslc, 1), 0)
    imps = [jnp.where((blk == 0) | (blk == t_l // SLC_LEN) | (blk == t_l // SLC_LEN - 1), jnp.inf, imp)
            for imp, t_l in zip(imps, t_ls)]
    imps = [jnp.where(blk * SLC_LEN <= t_l, imp, -jnp.inf) for imp, t_l in zip(imps, t_ls)]
    ranks = [jnp.zeros((nslc, sq), F32) for _ in streams]
    for i in range(nslc):
        ranks = [rank + jnp.where((imp[i:i + 1, :] > imp) | ((imp[i:i + 1, :] == imp) & (blk > i)), 1.0, 0.0)
                 for rank, imp in zip(ranks, imps)]
    for st, rank in zip(streams, ranks):
        sel_s[st * nslc:(st + 1) * nslc, :] = jnp.where(rank < float(min(SLC_TOP, nslc)), 1.0, 0.0)

    key_sub = lax.broadcasted_iota(jnp.int32, (tk, 1), 0)
    slc_sub = lax.broadcasted_iota(jnp.int32, (SLC_LEN, 1), 0)

    units = [(st, hd) for st in streams for hd in heads]

    def attend(kb, carries, jobs):
        k0 = pl.multiple_of(kb * tk, tk)
        work = []
        for j, (k_ref, vt_ref, bias_fn) in enumerate(jobs):
            kblk = k_ref[pl.ds(k0, tk), :]
            vblk_t = vt_ref[:, pl.ds(k0, tk)]
            biases = bias_fn(k0)
            work += [(j, u, kblk, vblk_t, biases[st], qts[st][:, hd]) for u, (st, hd) in enumerate(units)]
        carries = [list(c) for c in carries]

        def scores(w):
            return jnp.dot(work[w][2], work[w][5], preferred_element_type=F32)

        s_next = scores(0)
        for w, (j, u, _, vblk_t, bias, _) in enumerate(work):
            sh = s_next + bias
            if w + 1 < len(work):
                s_next = scores(w + 1)
            m_i, l_i, acc = carries[j][u]
            m_new = jnp.maximum(m_i, jnp.max(sh, axis=0, keepdims=True))
            alpha = jnp.exp2(m_i - m_new)
            p = jnp.exp2(sh - m_new)
            l_new = alpha * l_i + jnp.sum(p, axis=0, keepdims=True)
            pv = jnp.dot(vblk_t, p.astype(BF16), preferred_element_type=F32)
            carries[j][u] = (m_new, l_new, alpha * acc + pv)
        return tuple(tuple(c) for c in carries)

    init = tuple((jnp.full((1, sq), NEG, F32), jnp.zeros((1, sq), F32), jnp.zeros((HEAD_DIM, sq), F32))
                 for _ in units)

    def slc_bias(k0):
        out = []
        for st, t_l in zip(streams, t_ls):
            pieces = []
            for i in range(tk // SLC_LEN):
                picked = sel_s[pl.ds(st * nslc + k0 // SLC_LEN + i, 1), :]
                kpos = k0 + i * SLC_LEN + slc_sub
                pieces.append(jnp.where((picked > 0.5) & (kpos <= t_l), 0.0, NEG))
            out.append(jnp.concatenate(pieces, axis=0))
        return out

    def win_bias(k0):
        kpos = k0 + key_sub
        return [jnp.where((kpos <= t_l) & (kpos > t_l - WINDOW), 0.0, NEG) for t_l in t_ls]

    n_blocks = (t0 + tq + tk - 1) // tk
    win_lo = jnp.maximum(t0 - WINDOW + 1, 0) // tk
    slc_job = (ks_ref, vs_ref, slc_bias)
    win_job = (kw_ref, vw_ref, win_bias)
    (slc,) = lax.fori_loop(0, win_lo, lambda kb, c: attend(kb, c, [slc_job]), (init,))
    slc, win = lax.fori_loop(win_lo, n_blocks, lambda kb, c: attend(kb, c, [slc_job, win_job]), (slc, init))
    o_slcs = [acc * (1.0 / l) for _, l, acc in slc]
    o_wins = [acc * (1.0 / l) for _, l, acc in win]

    gate_row = lax.broadcasted_iota(jnp.int32, (SM_W, sq), 0)
    for st in streams:
        rs = slice(st * sq, (st + 1) * sq)
        sm_t = jnp.transpose(sm_ref[rs, :])

        def gate(col):
            return jax.nn.sigmoid(jnp.sum(jnp.where(gate_row == col, sm_t, 0.0), axis=0, keepdims=True))

        for hh, hd in enumerate(heads):
            col = GATE_COL0 + (g * hpg + hh) * 3
            u = st * hpg + hh
            o_t = (gate(col) * o_cmps[st][:, hd] + gate(col + 1) * o_slcs[u]
                   + gate(col + 2) * o_wins[u])
            o_ref[rs, hh * HEAD_DIM:(hh + 1) * HEAD_DIM] = jnp.transpose(o_t).astype(o_ref.dtype)


def _nsa_attn(p_nsa, p_sm, cosf, sins, kcmp, vcmp, ksr, vsb, kwr, vwb):
    bsz, seq, _ = p_nsa.shape
    ng = NSA_KV
    tq = NSA_TQ
    assert seq % tq == 0 and seq // CMP_STRIDE == LANES
    ncmp = seq // CMP_STRIDE
    gw = NSA_HPG * HEAD_DIM
    per_group = lambda rows: pl.BlockSpec((None, None, rows, LANES), lambda b, g, t: (b, g, 0, 0))
    per_group_t = pl.BlockSpec((None, None, LANES, seq), lambda b, g, t: (b, g, 0, 0))
    return pl.pallas_call(
        functools.partial(_nsa_attn_kernel, seq=seq),
        grid=(bsz, ng, seq // tq),
        in_specs=[pl.BlockSpec((None, tq, gw), lambda b, g, t: (b, t, g)),
                  pl.BlockSpec((tq, LANES), lambda b, g, t: (t, 0)),
                  pl.BlockSpec((tq, LANES), lambda b, g, t: (t, 0)),
                  per_group(ncmp), per_group(ncmp),
                  per_group(seq), per_group_t, per_group(seq), per_group_t,
                  pl.BlockSpec((None, tq, SM_W), lambda b, g, t: (b, t, 0))],
        out_specs=pl.BlockSpec((None, tq, gw), lambda b, g, t: (b, t, g)),
        out_shape=jax.ShapeDtypeStruct((bsz, seq, NSA_W), BF16),
        scratch_shapes=[pltpu.VMEM((tq // NSA_SQ * (seq // SLC_LEN), NSA_SQ), F32)],
        compiler_params=_params(("parallel", "parallel", "arbitrary"), VMEM_MIXER_MIB),
        name="nsa_attn",
    )(p_nsa, cosf, sins, kcmp, vcmp, ksr, vsb, kwr, vwb, p_sm)


def _rope_tables(seq):
    inv = ROPE_THETA ** (-jnp.arange(0, HEAD_DIM, 2, dtype=F32) / HEAD_DIM)
    ang = jnp.arange(seq, dtype=F32)[:, None] * inv[None, :]
    cos, sin = jnp.cos(ang), jnp.sin(ang)
    return jnp.concatenate([cos, cos], axis=-1), jnp.concatenate([-sin, sin], axis=-1)


def _mixers(p_hg, p_gd, p_nsa, p_sm, lb, hg_norm, conv_w, a_log, dt_bias, gd_norm,
            posk, w1k, w2k, posv, w1v, w2v, cosf, sins, w_down, layer):
    o_a, o_b, w_dn = _hgrn_gdn(p_hg, lb, hg_norm, p_gd, p_sm, conv_w, a_log, dt_bias, gd_norm, w_down, layer)
    prep = _nsa_prep(p_nsa, cosf, sins, posk, w1k.astype(BF16), w2k.astype(BF16),
                     posv, w1v.astype(BF16), w2v.astype(BF16))
    o_c = _nsa_attn(p_nsa, p_sm, cosf, sins, *prep)
    return (o_a, o_b, o_c), w_dn


def kernel(x, attn_norm, w_in, w_out, ffn_norm, w_gate, w_up, w_down, final_norm, hgrn_lb_logits, hgrn_out_norm, gdn_conv, gdn_A_log, gdn_dt_bias, gdn_out_norm, cmp_pos_k, cmp_w1_k, cmp_w2_k, cmp_pos_v, cmp_w1_v, cmp_w2_v):
    bsz, seq, d = x.shape
    depth = w_in.shape[0]
    n = bsz * seq
    cosf, sins = _rope_tables(seq)
    lb_p = jax.nn.softmax(hgrn_lb_logits.astype(F32), axis=0)
    lb_all = jnp.concatenate([jnp.zeros_like(lb_p[:1]), jnp.cumsum(lb_p, axis=0)[:-1]], axis=0)

    c_gd = 4 * HG_W
    c_ab = c_gd + 4 * GD_W
    c_nsa = c_ab + 2 * GD_HEADS
    c_gate = c_nsa + NSA_W + 6 * NSA_KV_W
    n_gate = NSA_HEADS * 3

    w_t = jnp.swapaxes(w_in, 1, 2)
    w_nsa = w_t[:, c_nsa:c_gate, :]
    w_sm = jnp.concatenate(
        [w_t[:, c_ab:c_nsa, :], w_t[:, c_gate:, :],
         jnp.zeros((depth, SM_W - 2 * GD_HEADS - n_gate, d), F32)], axis=1)

    xf = x.reshape(n, d)
    xw, ss = _norm_inputs(xf, attn_norm[0])
    for l in range(depth):
        p_hg = _matmul_nt(xw, ss, w_t, l, 0, c_gd).reshape(bsz, seq, -1)
        p_gd = _matmul_nt(xw, ss, w_t, l, c_gd, c_ab - c_gd).reshape(bsz, seq, -1)
        p_nsa = _matmul_nt(xw, ss, w_nsa, l).reshape(bsz, seq, -1)
        p_sm = _matmul_nt(xw, ss, w_sm, l).reshape(bsz, seq, -1)
        mix, w_dn = _mixers(p_hg, p_gd, p_nsa, p_sm, lb_all[l], hgrn_out_norm[l], gdn_conv[l],
                            gdn_A_log[l], gdn_dt_bias[l], gdn_out_norm[l],
                            cmp_pos_k[l], cmp_w1_k[l], cmp_w2_k[l], cmp_pos_v[l], cmp_w1_v[l], cmp_w2_v[l],
                            cosf, sins, w_down, l)
        xf, xw, ss = _out_proj([o.reshape(n, o.shape[-1]) for o in mix], w_out, l, xf, ffn_norm[l])
        act = _gate_up(xw, ss, w_gate, w_up, l)
        next_norm = attn_norm[l + 1] if l + 1 < depth else final_norm
        xf, xw, ss = _matmul_ktiled_res(act, w_dn[None], 0, xf, next_norm)
    return _rmsnorm(xf, final_norm, F32).reshape(bsz, seq, d)
```

```python
import functools

import jax
import jax.numpy as jnp
from jax import lax
from jax.experimental import pallas as pl
from jax.experimental.pallas import tpu as pltpu

F32 = jnp.float32
BF16 = jnp.bfloat16

HEAD_DIM = 128
HG_HEADS = 8
GD_HEADS = 8
CONV_W = 4
NSA_HEADS = 16
NSA_KV = 4
NSA_HPG = NSA_HEADS // NSA_KV
CMP_LEN = 32
CMP_STRIDE = 16
CMP_HIDDEN = 256
SLC_LEN = 64
SLC_TOP = 16
WINDOW = 512
ROPE_THETA = 10000.0
NORM_EPS = 1e-6
LB_FLOOR = 1e-30
NEG = -1e30
LOG2E = 1.4426950408889634
FFN_HIDDEN = 11008

HG_W = HG_HEADS * HEAD_DIM
GD_W = GD_HEADS * HEAD_DIM
NSA_W = NSA_HEADS * HEAD_DIM
NSA_KV_W = NSA_KV * HEAD_DIM
SM_W = 128
GATE_COL0 = 2 * GD_HEADS

LANES = 128
MIB = 1 << 20

MM_TM, MM_TN = 1024, 512
FFN_TN = 256
DOWN_TK = 5504
NORM_ROWS = 256
VMEM_MATMUL_MIB = 56
VMEM_DOWN_MIB = 48
VMEM_ROWWISE_MIB = 40
VMEM_MIXER_MIB = 48

NT = (((1,), (1,)), ((), ()))
TN = (((0,), (0,)), ((), ()))


def _params(sem, vmem_mib):
    return pltpu.CompilerParams(dimension_semantics=sem, vmem_limit_bytes=vmem_mib * MIB)


def _split2(x):
    hi = x.astype(BF16)
    return hi, (x - hi.astype(F32)).astype(BF16)


def _dot_x3(a, b):
    ah, al = _split2(a)
    bh, bl = _split2(b)
    d = lambda x, y: jnp.dot(x, y, preferred_element_type=F32)
    return d(ah, bh) + (d(ah, bl) + d(al, bh))


def _dot_exact_lhs(a, b):
    hi = b.astype(BF16)
    r = b - hi.astype(F32)
    mid = r.astype(BF16)
    lo = (r - mid.astype(F32)).astype(BF16)
    d = lambda y: jnp.dot(a, y, preferred_element_type=F32)
    return d(hi) + (d(mid) + d(lo))


def _silu(x):
    return x * jax.nn.sigmoid(x)


def _softplus(x):
    return jnp.maximum(x, 0.0) + jnp.log1p(jnp.exp(-jnp.abs(x)))


def _lane_pick(x, lane_ids, idx):
    return jnp.sum(jnp.where(lane_ids == idx, x, 0.0), axis=-1, keepdims=True)


def _rms_kernel(x_ref, w_ref, o_ref):
    x = x_ref[...]
    ms = jnp.mean(x * x, axis=-1, keepdims=True)
    o_ref[...] = (x * lax.rsqrt(ms + NORM_EPS) * w_ref[...]).astype(o_ref.dtype)


def _rmsnorm(x, w, out_dtype, rows=NORM_ROWS):
    n, d = x.shape
    return pl.pallas_call(
        _rms_kernel,
        grid=(n // rows,),
        in_specs=[pl.BlockSpec((rows, d), lambda i: (i, 0)),
                  pl.BlockSpec((1, d), lambda i: (0, 0))],
        out_specs=pl.BlockSpec((rows, d), lambda i: (i, 0)),
        out_shape=jax.ShapeDtypeStruct((n, d), out_dtype),
        compiler_params=_params(("parallel",), VMEM_ROWWISE_MIB),
        name="rmsnorm",
    )(x, w.reshape(1, d))


def _rinv(ss_ref, d):
    return lax.rsqrt(ss_ref[...] * (1.0 / d) + NORM_EPS)


def _emit_norm_inputs(x, nw_ref, xw_ref, ss_ref, first):
    xw_ref[...] = (x * nw_ref[...]).astype(xw_ref.dtype)
    part = jnp.sum(x * x, axis=-1, keepdims=True)

    @pl.when(first)
    def _():
        ss_ref[...] = part

    @pl.when(jnp.logical_not(first))
    def _():
        ss_ref[...] += part


def _norm_inputs_kernel(x_ref, w_ref, xw_ref, ss_ref):
    x = x_ref[...]
    xw_ref[...] = (x * w_ref[...]).astype(xw_ref.dtype)
    ss_ref[...] = jnp.sum(x * x, axis=-1, keepdims=True)


def _norm_inputs(x, w, rows=NORM_ROWS):
    n, d = x.shape
    return pl.pallas_call(
        _norm_inputs_kernel,
        grid=(n // rows,),
        in_specs=[pl.BlockSpec((rows, d), lambda i: (i, 0)),
                  pl.BlockSpec((1, d), lambda i: (0, 0))],
        out_specs=[pl.BlockSpec((rows, d), lambda i: (i, 0)), pl.BlockSpec((rows, 1), lambda i: (i, 0))],
        out_shape=[jax.ShapeDtypeStruct((n, d), BF16), jax.ShapeDtypeStruct((n, 1), F32)],
        compiler_params=_params(("parallel",), VMEM_ROWWISE_MIB),
        name="norm_inputs",
    )(x, w.reshape(1, d))


def _mm_nt_kernel(a_ref, b_ref, ss_ref, o_ref):
    acc = lax.dot_general(a_ref[...], b_ref[...].astype(BF16), NT, preferred_element_type=F32)
    o_ref[...] = _rinv(ss_ref, a_ref.shape[1]) * acc


def _matmul_nt(xw, ss, w_t, layer, row0=0, n=None, tm=MM_TM, tn=MM_TN):
    m, k = xw.shape
    n = w_t.shape[1] if n is None else n
    tn = min(tn, n)
    assert m % tm == 0 and n % tn == 0 and row0 % tn == 0
    rb = row0 // tn

    def pipelined(a_hbm, w_hbm, ss_hbm, o_hbm):
        pltpu.emit_pipeline(
            _mm_nt_kernel,
            grid=(m // tm, n // tn),
            in_specs=[pl.BlockSpec((tm, k), lambda i, j: (i, 0)),
                      pl.BlockSpec((None, tn, k), lambda i, j: (layer, rb + j, 0), pipeline_mode=pl.Buffered(3)),
                      pl.BlockSpec((tm, 1), lambda i, j: (i, 0))],
            out_specs=[pl.BlockSpec((tm, tn), lambda i, j: (i, j))],
        )(a_hbm, w_hbm, ss_hbm, o_hbm)

    any_space = pl.BlockSpec(memory_space=pl.ANY)
    return pl.pallas_call(
        pipelined,
        in_specs=[any_space, any_space, any_space],
        out_specs=any_space,
        out_shape=jax.ShapeDtypeStruct((m, n), F32),
        compiler_params=pltpu.CompilerParams(vmem_limit_bytes=VMEM_MATMUL_MIB * MIB),
        name="matmul",
    )(xw, w_t, ss)


def _residual_outputs(m, n, tm, tn, index):
    specs = [pl.BlockSpec((tm, tn), index), pl.BlockSpec((tm, tn), index),
             pl.BlockSpec((tm, 1), lambda i, *_: (i, 0))]
    shapes = [jax.ShapeDtypeStruct((m, n), F32), jax.ShapeDtypeStruct((m, n), BF16),
              jax.ShapeDtypeStruct((m, 1), F32)]
    return specs, shapes


def _out_proj_kernel(a_ref, b_ref, c_ref, w_ref, r_ref, nw_ref, o_ref, xw_ref, ss_ref):
    ka, kb = a_ref.shape[1], b_ref.shape[1]
    acc = r_ref[...] + jnp.dot(a_ref[...], w_ref[0:ka, :].astype(BF16), preferred_element_type=F32)
    acc = acc + jnp.dot(b_ref[...], w_ref[ka:ka + kb, :].astype(BF16), preferred_element_type=F32)
    x = acc + jnp.dot(c_ref[...], w_ref[ka + kb:, :].astype(BF16), preferred_element_type=F32)
    o_ref[...] = x
    _emit_norm_inputs(x, nw_ref, xw_ref, ss_ref, pl.program_id(1) == 0)


def _out_proj(parts, w, layer, res, next_norm_w, tm=MM_TM, tn=MM_TN):
    m, n = res.shape
    k = w.shape[1]
    tile = lambda i, j: (i, j)
    out_specs, out_shape = _residual_outputs(m, n, tm, tn, tile)
    return pl.pallas_call(
        _out_proj_kernel,
        grid=(m // tm, n // tn),
        in_specs=[pl.BlockSpec((tm, p.shape[1]), lambda i, j: (i, 0)) for p in parts]
        + [pl.BlockSpec((None, k, tn), lambda i, j: (layer, 0, j)),
           pl.BlockSpec((tm, tn), tile), pl.BlockSpec((1, tn), lambda i, j: (0, j))],
        out_specs=out_specs,
        out_shape=out_shape,
        compiler_params=_params(("parallel", "arbitrary"), VMEM_MATMUL_MIB),
        name="out_proj",
    )(*parts, w, res, next_norm_w.reshape(1, n))


def _mm_acc_res_kernel(a_ref, b_ref, r_ref, nw_ref, o_ref, xw_ref, ss_ref, acc_ref, *, nk):
    k = pl.program_id(2)

    @pl.when(k == 0)
    def _():
        acc_ref[...] = r_ref[...]

    acc_ref[...] += jnp.dot(a_ref[...], b_ref[...], preferred_element_type=F32)

    @pl.when(k == nk - 1)
    def _():
        x = acc_ref[...]
        o_ref[...] = x
        _emit_norm_inputs(x, nw_ref, xw_ref, ss_ref, pl.program_id(1) == 0)


def _matmul_ktiled_res(a, b, layer, res, next_norm_w, tm=MM_TM, tn=MM_TN, tk=DOWN_TK):
    m, k = a.shape
    n = b.shape[2]
    nk = k // tk
    tile = lambda i, j, kk: (i, j)
    out_specs, out_shape = _residual_outputs(m, n, tm, tn, tile)
    return pl.pallas_call(
        functools.partial(_mm_acc_res_kernel, nk=nk),
        grid=(m // tm, n // tn, nk),
        in_specs=[pl.BlockSpec((tm, tk), lambda i, j, kk: (i, kk)),
                  pl.BlockSpec((None, tk, tn), lambda i, j, kk: (layer, kk, j)),
                  pl.BlockSpec((tm, tn), tile), pl.BlockSpec((1, tn), lambda i, j, kk: (0, j))],
        out_specs=out_specs,
        out_shape=out_shape,
        scratch_shapes=[pltpu.VMEM((tm, tn), F32)],
        compiler_params=_params(("parallel", "arbitrary", "arbitrary"), VMEM_DOWN_MIB),
        name="matmul_down",
    )(a, b, res, next_norm_w.reshape(1, n))


def _gate_up_kernel(a_ref, wg_ref, wu_ref, ss_ref, o_ref):
    a = a_ref[...]
    rinv = _rinv(ss_ref, a_ref.shape[1])
    g = rinv * jnp.dot(a, wg_ref[...].astype(BF16), preferred_element_type=F32)
    u = rinv * jnp.dot(a, wu_ref[...].astype(BF16), preferred_element_type=F32)
    o_ref[...] = (_silu(g) * u).astype(o_ref.dtype)


def _gate_up(xw, ss, wg, wu, layer, tm=MM_TM, tn=FFN_TN):
    m, k = xw.shape
    n = wg.shape[2]
    return pl.pallas_call(
        _gate_up_kernel,
        grid=(m // tm, n // tn),
        in_specs=[pl.BlockSpec((tm, k), lambda i, j: (i, 0)),
                  pl.BlockSpec((None, k, tn), lambda i, j: (layer, 0, j)),
                  pl.BlockSpec((None, k, tn), lambda i, j: (layer, 0, j)),
                  pl.BlockSpec((tm, 1), lambda i, j: (i, 0))],
        out_specs=pl.BlockSpec((tm, tn), lambda i, j: (i, j)),
        out_shape=jax.ShapeDtypeStruct((m, n), BF16),
        compiler_params=_params(("parallel", "parallel"), VMEM_MATMUL_MIB),
        name="ffn_gate_up",
    )(xw, wg, wu, ss)


HG_ROWS = 64
HG_SUB = 16
HG_GROUP = 2


def _hgrn_phases(q_ref, f_ref, i_ref, g_ref, lb_ref, nw_ref, o_ref):
    rows, sub = HG_ROWS, HG_SUB
    row_id = lax.broadcasted_iota(jnp.int32, (rows, LANES), 0)
    sub_id = row_id & (sub - 1)
    r2 = lax.broadcasted_iota(jnp.int32, (rows, rows), 0)
    c2 = lax.broadcasted_iota(jnp.int32, (rows, rows), 1)
    tri = jnp.where(((r2 // sub) == (c2 // sub)) & (c2 <= r2), 1.0, 0.0).astype(BF16)
    log_lb = jnp.log(jnp.maximum(lb_ref[...], LB_FLOOR))
    nw = nw_ref[...]
    nsub = rows // sub

    def features(c):
        r0 = pl.multiple_of(c * rows, rows)
        q = _silu(q_ref[pl.ds(r0, rows), :])
        fp = f_ref[pl.ds(r0, rows), :]
        v = i_ref[pl.ds(r0, rows), :]
        logf = jnp.minimum(-_softplus(-fp) + _softplus(log_lb - fp), 0.0)
        return q, 1.0 - jnp.exp(logf), v, logf

    def phases(i, box):
        cs = [i * HG_GROUP + j for j in range(HG_GROUP)]
        d_ = {}

        def load():
            d_["feats"] = [features(c) for c in cs]

        def cumsum():
            d_["bs"] = [_dot_exact_lhs(tri, f[3]) * LOG2E for f in d_["feats"]]

        def intra():
            feats, bs = d_["feats"], d_["bs"]
            accs = [jnp.sum(q * kf, axis=-1, keepdims=True) * v for q, kf, v, _ in feats]
            for d in range(1, sub):
                prods = [jnp.where(sub_id >= d, q * pltpu.roll(kf, d, 0)
                                   * jnp.exp2(jnp.minimum(b - pltpu.roll(b, d, 0), 0.0)), 0.0)
                         for (q, kf, _, _), b in zip(feats, bs)]
                accs = [acc + jnp.sum(p, axis=-1, keepdims=True) * pltpu.roll(f[2], d, 0)
                        for acc, p, f in zip(accs, prods, feats)]
            d_["accs"] = accs

        def products():
            pieces = []
            for (q, kf, v, _), b in zip(d_["feats"], d_["bs"]):
                for j in range(nsub):
                    sl = slice(j * sub, (j + 1) * sub)
                    bj = b[sl]
                    bl = bj[sub - 1:sub]
                    kh = (kf[sl] * jnp.exp2(bl - bj)).astype(BF16)
                    vk = lax.dot_general(v[sl].astype(BF16), kh, TN, preferred_element_type=F32)
                    pieces.append(((q[sl] * jnp.exp2(bj)).astype(BF16), jnp.exp2(bl), vk))
            d_["pieces"] = pieces

        def chain():
            state, inter = box["s"], []
            for qt, decay, vk in d_["pieces"]:
                inter.append(lax.dot_general(qt, state.astype(BF16), NT, preferred_element_type=F32))
                state = decay * state + vk
            box["s"], d_["inter"] = state, inter

        def emit():
            for n, c in enumerate(cs):
                r0 = pl.multiple_of(c * rows, rows)
                o = d_["accs"][n] + jnp.concatenate(d_["inter"][n * nsub:(n + 1) * nsub], axis=0)
                ms = jnp.mean(o * o, axis=-1, keepdims=True)
                y = o * lax.rsqrt(ms + NORM_EPS) * nw
                y = y * _silu(g_ref[pl.ds(r0, rows), :])
                o_ref[pl.ds(r0, rows), :] = y.astype(o_ref.dtype)

        return [load, cumsum, intra, products, chain, emit]

    return phases


GD_CHUNK = 64
GD_GROUP = 8


def _gdn_program(q_ref, k_ref, v_ref, z_ref, sm_ref, cq_ref, ck_ref, cv_ref, al_ref, dtb_ref, nw_ref,
                 o_ref, am_s, bm_s, qp_s, op_s, dl_s):
    ch = GD_CHUNK
    h = pl.program_id(1)
    lane1 = lax.broadcasted_iota(jnp.int32, (1, LANES), 1)
    lane = lax.broadcasted_iota(jnp.int32, (ch, LANES), 1)
    neg_a = -jnp.exp(_lane_pick(al_ref[...], lane1, h))
    dt_bias = _lane_pick(dtb_ref[...], lane1, h)
    r2 = lax.broadcasted_iota(jnp.int32, (ch, ch), 0)
    c2 = lax.broadcasted_iota(jnp.int32, (ch, ch), 1)
    incl = c2 <= r2
    strict = c2 < r2
    tri = jnp.where(incl, 1.0, 0.0).astype(BF16)
    eye = jnp.where(c2 == r2, 1.0, 0.0).astype(F32)
    nw = nw_ref[...]

    def conv_silu(ref, w_ref, r0, c):
        cur = ref[pl.ds(r0, ch), :]
        prev = ref[pl.ds(jnp.maximum(r0 - 8, 0), 8), :]
        prev = jnp.where(c > 0, prev, 0.0)
        x = jnp.concatenate([prev, cur], axis=0)
        w = w_ref[...]
        y = cur * w[CONV_W - 1:CONV_W]
        for s in range(1, CONV_W):
            y = y + pltpu.roll(x, s, 0)[8:] * w[CONV_W - 1 - s:CONV_W - s]
        return _silu(y)

    def features(c):
        r0 = pl.multiple_of(c * ch, ch)
        qf = conv_silu(q_ref, cq_ref, r0, c)
        kf = conv_silu(k_ref, ck_ref, r0, c)
        vf = conv_silu(v_ref, cv_ref, r0, c)
        qf = qf * lax.rsqrt(jnp.sum(qf * qf, axis=-1, keepdims=True) + NORM_EPS) * (HEAD_DIM ** -0.5)
        kf = kf * lax.rsqrt(jnp.sum(kf * kf, axis=-1, keepdims=True) + NORM_EPS)
        sm = sm_ref[pl.ds(r0, ch), :]
        a = _lane_pick(sm, lane, h)
        beta = jax.nn.sigmoid(_lane_pick(sm, lane, h + GD_HEADS))
        g = neg_a * _softplus(a + dt_bias)
        return qf, kf, vf, beta, jnp.broadcast_to(g, (ch, LANES))

    def stages(i):
        cs = [i * GD_GROUP + j for j in range(GD_GROUP)]
        d_ = {}

        def load():
            d_["feats"] = [features(c) for c in cs]

        def cumsum():
            d_["gcs"] = [_dot_exact_lhs(tri, f[4]) for f in d_["feats"]]

        def gram():
            feats = d_["feats"]
            d_["decays"] = [jnp.where(incl, jnp.exp(jnp.minimum(gc[:, :ch] - jnp.transpose(gc)[:ch, :], 0.0)), 0.0)
                            for gc in d_["gcs"]]
            kbs = [f[1].astype(BF16) for f in feats]
            kks = [lax.dot_general(kb, kb, NT, preferred_element_type=F32) for kb in kbs]
            d_["qks"] = [lax.dot_general(f[0].astype(BF16), kb, NT, preferred_element_type=F32)
                         for f, kb in zip(feats, kbs)]
            d_["pws"] = [jnp.where(strict, -(f[3] * kk * d), 0.0) for f, kk, d in zip(feats, kks, d_["decays"])]
            d_["invs"] = [eye + pw for pw in d_["pws"]]

        def square():
            d_["pws"] = [_dot_x3(pw, pw) for pw in d_["pws"]]
            d_["invs"] = [inv + _dot_x3(inv, pw) for inv, pw in zip(d_["invs"], d_["pws"])]

        def solve():
            d_["egcs"] = [jnp.exp(gc) for gc in d_["gcs"]]
            d_["wus"] = [_dot_x3(inv, jnp.concatenate([f[1] * (f[3] * egc), f[2] * f[3]], axis=1)).astype(BF16)
                         for inv, f, egc in zip(d_["invs"], d_["feats"], d_["egcs"])]

        def maps():
            d_["qos"] = [jnp.dot(jnp.where(incl, qk * d, 0.0).astype(BF16), wu, preferred_element_type=F32)
                         for qk, d, wu in zip(d_["qks"], d_["decays"], d_["wus"])]
            d_["abs"] = [lax.dot_general((f[1] * jnp.exp(gc[ch - 1:ch] - gc)).astype(BF16), wu, TN,
                                         preferred_element_type=F32)
                         for f, gc, wu in zip(d_["feats"], d_["gcs"], d_["wus"])]

        def store():
            for c, f, gc, egc, qo, ab in zip(cs, d_["feats"], d_["gcs"], d_["egcs"], d_["qos"], d_["abs"]):
                r0 = pl.multiple_of(c * ch, ch)
                m0 = pl.multiple_of(c * LANES, LANES)
                am_s[pl.ds(m0, LANES), :] = ab[:, :LANES].astype(BF16)
                bm_s[pl.ds(m0, LANES), :] = ab[:, LANES:]
                qp_s[pl.ds(r0, ch), :] = (f[0] * egc - qo[:, :LANES]).astype(BF16)
                op_s[pl.ds(r0, ch), :] = qo[:, LANES:]
                dl_s[pl.ds(pl.multiple_of(c * 8, 8), 8), :] = jnp.broadcast_to(jnp.exp(gc[ch - 1:ch]), (8, LANES))

        return [load, cumsum, gram] + [square] * 5 + [solve, maps, store]

    def recur(c, state):
        r0 = pl.multiple_of(c * ch, ch)
        m0 = pl.multiple_of(c * LANES, LANES)
        sb = state.astype(BF16)
        o = jnp.dot(qp_s[pl.ds(r0, ch), :], sb, preferred_element_type=F32) + op_s[pl.ds(r0, ch), :]
        dl = dl_s[pl.ds(pl.multiple_of(c * 8, 8), 1), :]
        state = (dl * state - jnp.dot(am_s[pl.ds(m0, LANES), :], sb, preferred_element_type=F32)
                 + bm_s[pl.ds(m0, LANES), :])
        ms = jnp.mean(o * o, axis=-1, keepdims=True)
        y = o * lax.rsqrt(ms + NORM_EPS) * nw
        y = y * _silu(z_ref[pl.ds(r0, ch), :])
        o_ref[pl.ds(r0, ch), :] = y.astype(o_ref.dtype)
        return state

    return stages, recur


_FUSED_ORDER = ("g0", (0, 0), "g1", (0, 1), "g2", (0, 2), "g3", (0, 3), (0, 4), (0, 5), (1, 0), (1, 1),
                "g4", (1, 2), "g5", (1, 3), (1, 4), (1, 5), (2, 0), (2, 1), "g6", (2, 2), "g7",
                (2, 3), (2, 4), (2, 5), (3, 0), (3, 1), "g8", (3, 2), "g9", (3, 3), (3, 4), (3, 5), "g10")


def _hgrn_gdn_kernel(*refs, seq):
    hg_refs, gd_refs = refs[:6] + refs[18:19], refs[6:17] + refs[19:20] + refs[21:]
    wd_in, wd_out = refs[17], refs[20]
    phases = _hgrn_phases(*hg_refs)
    stages, recur = _gdn_program(*gd_refs)
    nchunks = seq // GD_CHUNK
    steps = GD_GROUP * GD_CHUNK // (HG_ROWS * HG_GROUP)
    wcols = wd_in.shape[1] // (nchunks // GD_GROUP)

    ngroups = nchunks // GD_GROUP
    every = len(_FUSED_ORDER) // GD_GROUP

    def body(i, carry, recur_previous):
        box = {"s": carry[0]}
        gstate = carry[1]
        cols = pl.ds(pl.multiple_of(i * wcols, wcols), wcols)
        wd_out[:, cols] = wd_in[:, cols].astype(wd_out.dtype)
        gd = stages(i)
        hg = [phases(i * steps + k, box) for k in range(steps)]
        for n, item in enumerate(_FUSED_ORDER):
            if isinstance(item, str):
                gd[int(item[1:])]()
            else:
                hg[item[0]][item[1]]()
            if recur_previous and n % every == every - 1 and n // every < GD_GROUP:
                gstate = recur((i - 1) * GD_GROUP + n // every, gstate)
        return box["s"], gstate

    zero = jnp.zeros((LANES, LANES), F32)
    carry = body(jnp.int32(0), (zero, zero), False)
    carry = lax.fori_loop(1, ngroups, lambda i, c: body(i, c, True), carry)
    lax.fori_loop(nchunks - GD_GROUP, nchunks, recur, carry[1], unroll=2)


def _hgrn_gdn(p_hg, lb, hg_norm, p_gd, p_sm, conv_w, a_log, dt_bias, gd_norm, w_cast, layer):
    bsz, seq, _ = p_hg.shape
    nh = HG_HEADS
    nchunks = seq // GD_CHUNK
    assert HG_HEADS == GD_HEADS and seq % (GD_CHUNK * GD_GROUP) == 0
    assert (GD_GROUP * GD_CHUNK) % (HG_ROWS * HG_GROUP) == 0 and len(_FUSED_ORDER) == 11 + 6 * 4
    wk, wn = w_cast.shape[1:]
    wrows, wcols, col_blocks = wk // (2 * bsz), 2 * wn // nh, nh // 2
    assert wk % (2 * bsz) == 0 and wrows % 16 == 0 and wcols % (LANES * (nchunks // GD_GROUP)) == 0
    w_block = lambda b, h: (b * 2 + h // col_blocks, h % col_blocks)

    def part(k):
        return pl.BlockSpec((None, seq, LANES), lambda b, h, k=k: (b, 0, h + nh * k))

    def cpart(k):
        return pl.BlockSpec((CONV_W, LANES), lambda b, h, k=k: (0, h + nh * k))

    pad = lambda t: jnp.pad(t.astype(F32), (0, LANES - nh)).reshape(1, LANES)
    small = pl.BlockSpec((1, LANES), lambda b, h: (0, 0))
    head_out = pl.BlockSpec((None, seq, LANES), lambda b, h: (b, 0, h))
    return pl.pallas_call(
        functools.partial(_hgrn_gdn_kernel, seq=seq),
        grid=(bsz, nh),
        in_specs=[part(0), part(1), part(2), part(3), pl.BlockSpec((1, LANES), lambda b, h: (0, h)), small,
                  part(0), part(1), part(2), part(3),
                  pl.BlockSpec((None, seq, SM_W), lambda b, h: (b, 0, 0)),
                  cpart(0), cpart(1), cpart(2), small, small, small,
                  pl.BlockSpec((None, wrows, wcols), lambda b, h: (layer,) + w_block(b, h))],
        out_specs=[head_out, head_out, pl.BlockSpec((wrows, wcols), w_block)],
        out_shape=[jax.ShapeDtypeStruct((bsz, seq, HG_W), BF16), jax.ShapeDtypeStruct((bsz, seq, GD_W), BF16),
                   jax.ShapeDtypeStruct((wk, wn), BF16)],
        scratch_shapes=[pltpu.VMEM((nchunks * LANES, LANES), BF16), pltpu.VMEM((nchunks * LANES, LANES), F32),
                        pltpu.VMEM((seq, LANES), BF16), pltpu.VMEM((seq, LANES), F32),
                        pltpu.VMEM((nchunks * 8, LANES), F32)],
        compiler_params=_params(("parallel", "parallel"), VMEM_MATMUL_MIB),
        name="hgrn_gdn",
    )(p_hg, p_hg, p_hg, p_hg, lb.reshape(1, HG_W), hg_norm.reshape(1, LANES),
      p_gd, p_gd, p_gd, p_gd, p_sm, conv_w, conv_w, conv_w, pad(a_log), pad(dt_bias),
      gd_norm.reshape(1, LANES), w_cast)


NSA_TQ = 256
NSA_SQ = 128
NSA_TK = 256
ROPE_ROWS = 256


def _gelu_tanh(x):
    return 0.5 * x * (1.0 + jnp.tanh(0.7978845608028654 * (x + 0.044715 * x * x * x)))


def _nsa_prep_kernel(kc_ref, vc_ref, ks_ref, vs_ref, kw_ref, vw_ref, cos_ref, sin_ref,
                     posk_ref, w1k_ref, w2k_ref, posv_ref, w1v_ref, w2v_ref,
                     kcmp_ref, vcmp_ref, ksr_ref, vsb_ref, kwr_ref, vwb_ref, kcr_s, *, seq):
    ncmp = seq // CMP_STRIDE
    per = CMP_LEN // CMP_STRIDE

    def rope_rows(i, carry):
        r0 = pl.multiple_of(i * ROPE_ROWS, ROPE_ROWS)
        rs = pl.ds(r0, ROPE_ROWS)
        cosf = cos_ref[rs, :]
        sins = sin_ref[rs, :]

        def rope(x):
            return x * cosf + pltpu.roll(x, HEAD_DIM // 2, 1) * sins

        kcr_s[rs, :] = rope(kc_ref[rs, :])
        ksr_ref[rs, :] = rope(ks_ref[rs, :]).astype(BF16)
        kwr_ref[rs, :] = rope(kw_ref[rs, :]).astype(BF16)
        vsb_ref[:, rs] = jnp.transpose(vs_ref[rs, :]).astype(BF16)
        vwb_ref[:, rs] = jnp.transpose(vw_ref[rs, :]).astype(BF16)
        return carry

    lax.fori_loop(0, seq // ROPE_ROWS, rope_rows, 0)

    def compress(src_ref, pos_ref, w1_ref, w2_ref, out_ref, transpose_out):
        parts = [jnp.zeros((ncmp, CMP_HIDDEN), F32) for _ in range(per)]
        for j in range(CMP_STRIDE):
            xj = src_ref[pl.ds(j, ncmp, stride=CMP_STRIDE), :]
            for p in range(per):
                jj = p * CMP_STRIDE + j
                xp = (xj + pos_ref[jj:jj + 1, :]).astype(BF16)
                parts[p] = parts[p] + jnp.dot(xp, w1_ref[jj * HEAD_DIM:(jj + 1) * HEAD_DIM, :],
                                              preferred_element_type=F32)
        pre = parts[0]
        for p in range(1, per):
            pre = pre + pltpu.roll(parts[p], ncmp - p, 0)
        hid = _gelu_tanh(pre)
        out = jnp.dot(hid.astype(BF16), w2_ref[...], preferred_element_type=F32)
        out_ref[...] = (jnp.transpose(out) if transpose_out else out).astype(BF16)

    compress(kcr_s, posk_ref, w1k_ref, w2k_ref, kcmp_ref, False)
    compress(vc_ref, posv_ref, w1v_ref, w2v_ref, vcmp_ref, True)


def _nsa_prep(p_nsa, cosf, sins, posk, w1k, w2k, posv, w1v, w2v):
    bsz, seq, _ = p_nsa.shape
    ng = NSA_KV
    ncmp = seq // CMP_STRIDE
    q_blocks = NSA_W // LANES

    def part(k):
        return pl.BlockSpec((None, seq, LANES), lambda b, g, k=k: (b, 0, q_blocks + ng * k + g))

    def whole(shape):
        return pl.BlockSpec(shape, lambda b, g: (0,) * len(shape))

    seq_out = pl.BlockSpec((None, None, seq, LANES), lambda b, g: (b, g, 0, 0))
    seq_out_t = pl.BlockSpec((None, None, LANES, seq), lambda b, g: (b, g, 0, 0))
    cmp_out = pl.BlockSpec((None, None, ncmp, LANES), lambda b, g: (b, g, 0, 0))
    seq_shape = jax.ShapeDtypeStruct((bsz, ng, seq, LANES), BF16)
    seq_shape_t = jax.ShapeDtypeStruct((bsz, ng, LANES, seq), BF16)
    cmp_shape = jax.ShapeDtypeStruct((bsz, ng, ncmp, LANES), BF16)
    return pl.pallas_call(
        functools.partial(_nsa_prep_kernel, seq=seq),
        grid=(bsz, ng),
        in_specs=[part(0), part(1), part(2), part(3), part(4), part(5),
                  whole((seq, LANES)), whole((seq, LANES)),
                  whole((CMP_LEN, HEAD_DIM)), whole((CMP_LEN * HEAD_DIM, CMP_HIDDEN)), whole((CMP_HIDDEN, HEAD_DIM)),
                  whole((CMP_LEN, HEAD_DIM)), whole((CMP_LEN * HEAD_DIM, CMP_HIDDEN)), whole((CMP_HIDDEN, HEAD_DIM))],
        out_specs=[cmp_out, cmp_out, seq_out, seq_out_t, seq_out, seq_out_t],
        out_shape=[cmp_shape, cmp_shape, seq_shape, seq_shape_t, seq_shape, seq_shape_t],
        scratch_shapes=[pltpu.VMEM((seq, LANES), F32)],
        compiler_params=_params(("parallel", "parallel"), VMEM_MIXER_MIB),
        name="nsa_prep",
    )(p_nsa, p_nsa, p_nsa, p_nsa, p_nsa, p_nsa, cosf, sins, posk, w1k, w2k, posv, w1v, w2v)


def _nsa_attn_kernel(q_ref, cos_ref, sin_ref, kcmp_ref, vcmp_ref, ks_ref, vs_ref, kw_ref, vw_ref,
                     sm_ref, o_ref, sel_s, *, seq):
    tq, sq, tk, hpg = NSA_TQ, NSA_SQ, NSA_TK, NSA_HPG
    streams = range(tq // sq)
    cols = hpg * sq
    scale = HEAD_DIM ** -0.5
    g = pl.program_id(1)
    t0 = pl.program_id(2) * tq
    ncmp = seq // CMP_STRIDE
    nslc = seq // SLC_LEN
    kcmp = kcmp_ref[...]
    vcmp_t = vcmp_ref[...]
    heads = [slice(hh * sq, (hh + 1) * sq) for hh in range(hpg)]

    def rope_q_t(st):
        rs = slice(st * sq, (st + 1) * sq)
        cosf = cos_ref[rs, :] * (scale * LOG2E)
        sins = sin_ref[rs, :] * (scale * LOG2E)
        out = []
        for hh in range(hpg):
            x = q_ref[rs, hh * HEAD_DIM:(hh + 1) * HEAD_DIM]
            out.append(jnp.transpose(x * cosf + pltpu.roll(x, HEAD_DIM // 2, 1) * sins).astype(BF16))
        return jnp.concatenate(out, axis=1)

    qts = [rope_q_t(st) for st in streams]
    t_ls = [t0 + st * sq + lax.broadcasted_iota(jnp.int32, (1, sq), 1) for st in streams]

    n_sub = lax.broadcasted_iota(jnp.int32, (ncmp, 1), 0)
    cmasks = [(n_sub * CMP_STRIDE + (CMP_LEN - 1)) <= t_l for t_l in t_ls]
    ss = [jnp.dot(kcmp, qt, preferred_element_type=F32) for qt in qts]
    ss = [[jnp.where(cm, s[:, hd], NEG) for hd in heads] for cm, s in zip(cmasks, ss)]
    es = [[jnp.exp2(s - jnp.max(s, axis=0, keepdims=True)) for s in sh] for sh in ss]
    p_cmps = [[jnp.where(cm, e * (1.0 / jnp.sum(e, axis=0, keepdims=True)), 0.0) for e in eh]
              for cm, eh in zip(cmasks, es)]
    o_cmps = [jnp.dot(vcmp_t, jnp.concatenate(ph, axis=1).astype(BF16), preferred_element_type=F32)
              for ph in p_cmps]

    j_i = lax.broadcasted_iota(jnp.int32, (nslc, ncmp), 0)
    n_i = lax.broadcasted_iota(jnp.int32, (nslc, ncmp), 1)
    ovl = (jnp.minimum(n_i * CMP_STRIDE + CMP_LEN, j_i * SLC_LEN + SLC_LEN)
           - jnp.maximum(n_i * CMP_STRIDE, j_i * SLC_LEN))
    ovl = (jnp.maximum(ovl, 0).astype(F32) * (1.0 / CMP_LEN)).astype(BF16)
    imps = [_dot_exact_lhs(ovl, functools.reduce(lambda a, b: a + b, ph)) for ph in p_cmps]
    blk = lax.broadcasted_iota(jnp.int32, (nslc, 1), 0)
    imps = [jnp.where((blk == 0) | (blk == t_l // SLC_LEN) | (blk == t_l // SLC_LEN - 1), jnp.inf, imp)
            for imp, t_l in zip(imps, t_ls)]
    imps = [jnp.where(blk * SLC_LEN <= t_l, imp, -jnp.inf) for imp, t_l in zip(imps, t_ls)]
    ranks = [jnp.zeros((nslc, sq), F32) for _ in streams]
    for i in range(nslc):
        ranks = [rank + jnp.where((imp[i:i + 1, :] > imp) | ((imp[i:i + 1, :] == imp) & (blk > i)), 1.0, 0.0)
                 for rank, imp in zip(ranks, imps)]
    for st, rank in zip(streams, ranks):
        sel_s[st * nslc:(st + 1) * nslc, :] = jnp.where(rank < float(min(SLC_TOP, nslc)), 1.0, 0.0)

    key_sub = lax.broadcasted_iota(jnp.int32, (tk, 1), 0)
    slc_sub = lax.broadcasted_iota(jnp.int32, (SLC_LEN, 1), 0)

    units = [(st, hd) for st in streams for hd in heads]

    def attend(kb, carries, jobs):
        k0 = pl.multiple_of(kb * tk, tk)
        work = []
        for j, (k_ref, vt_ref, bias_fn) in enumerate(jobs):
            kblk = k_ref[pl.ds(k0, tk), :]
            vblk_t = vt_ref[:, pl.ds(k0, tk)]
            biases = bias_fn(k0)
            work += [(j, u, kblk, vblk_t, biases[st], qts[st][:, hd]) for u, (st, hd) in enumerate(units)]
        carries = [list(c) for c in carries]

        def scores(w):
            return jnp.dot(work[w][2], work[w][5], preferred_element_type=F32)

        s_next = scores(0)
        for w, (j, u, _, vblk_t, bias, _) in enumerate(work):
            sh = s_next + bias
            if w + 1 < len(work):
                s_next = scores(w + 1)
            m_i, l_i, acc = carries[j][u]
            m_new = jnp.maximum(m_i, jnp.max(sh, axis=0, keepdims=True))
            alpha = jnp.exp2(m_i - m_new)
            p = jnp.exp2(sh - m_new)
            l_new = alpha * l_i + jnp.sum(p, axis=0, keepdims=True)
            pv = jnp.dot(vblk_t, p.astype(BF16), preferred_element_type=F32)
            carries[j][u] = (m_new, l_new, alpha * acc + pv)
        return tuple(tuple(c) for c in carries)

    init = tuple((jnp.full((1, sq), NEG, F32), jnp.zeros((1, sq), F32), jnp.zeros((HEAD_DIM, sq), F32))
                 for _ in units)

    def slc_bias(k0):
        out = []
        for st, t_l in zip(streams, t_ls):
            pieces = []
            for i in range(tk // SLC_LEN):
                picked = sel_s[pl.ds(st * nslc + k0 // SLC_LEN + i, 1), :]
                kpos = k0 + i * SLC_LEN + slc_sub
                pieces.append(jnp.where((picked > 0.5) & (kpos <= t_l), 0.0, NEG))
            out.append(jnp.concatenate(pieces, axis=0))
        return out

    def win_bias(k0):
        kpos = k0 + key_sub
        return [jnp.where((kpos <= t_l) & (kpos > t_l - WINDOW), 0.0, NEG) for t_l in t_ls]

    n_blocks = (t0 + tq + tk - 1) // tk
    win_lo = jnp.maximum(t0 - WINDOW + 1, 0) // tk
    slc_job = (ks_ref, vs_ref, slc_bias)
    win_job = (kw_ref, vw_ref, win_bias)
    (slc,) = lax.fori_loop(0, win_lo, lambda kb, c: attend(kb, c, [slc_job]), (init,))
    slc, win = lax.fori_loop(win_lo, n_blocks, lambda kb, c: attend(kb, c, [slc_job, win_job]), (slc, init))
    o_slcs = [acc * (1.0 / l) for _, l, acc in slc]
    o_wins = [acc * (1.0 / l) for _, l, acc in win]

    gate_row = lax.broadcasted_iota(jnp.int32, (SM_W, sq), 0)
    for st in streams:
        rs = slice(st * sq, (st + 1) * sq)
        sm_t = jnp.transpose(sm_ref[rs, :])

        def gate(col):
            return jax.nn.sigmoid(jnp.sum(jnp.where(gate_row == col, sm_t, 0.0), axis=0, keepdims=True))

        for hh, hd in enumerate(heads):
            col = GATE_COL0 + (g * hpg + hh) * 3
            u = st * hpg + hh
            o_t = (gate(col) * o_cmps[st][:, hd] + gate(col + 1) * o_slcs[u]
                   + gate(col + 2) * o_wins[u])
            o_ref[rs, hh * HEAD_DIM:(hh + 1) * HEAD_DIM] = jnp.transpose(o_t).astype(o_ref.dtype)


def _nsa_attn(p_nsa, p_sm, cosf, sins, kcmp, vcmp, ksr, vsb, kwr, vwb):
    bsz, seq, _ = p_nsa.shape
    ng = NSA_KV
    tq = NSA_TQ
    assert seq % tq == 0 and seq // CMP_STRIDE == LANES
    ncmp = seq // CMP_STRIDE
    gw = NSA_HPG * HEAD_DIM
    per_group = lambda rows: pl.BlockSpec((None, None, rows, LANES), lambda b, g, t: (b, g, 0, 0))
    per_group_t = pl.BlockSpec((None, None, LANES, seq), lambda b, g, t: (b, g, 0, 0))
    return pl.pallas_call(
        functools.partial(_nsa_attn_kernel, seq=seq),
        grid=(bsz, ng, seq // tq),
        in_specs=[pl.BlockSpec((None, tq, gw), lambda b, g, t: (b, t, g)),
                  pl.BlockSpec((tq, LANES), lambda b, g, t: (t, 0)),
                  pl.BlockSpec((tq, LANES), lambda b, g, t: (t, 0)),
                  per_group(ncmp), per_group(ncmp),
                  per_group(seq), per_group_t, per_group(seq), per_group_t,
                  pl.BlockSpec((None, tq, SM_W), lambda b, g, t: (b, t, 0))],
        out_specs=pl.BlockSpec((None, tq, gw), lambda b, g, t: (b, t, g)),
        out_shape=jax.ShapeDtypeStruct((bsz, seq, NSA_W), BF16),
        scratch_shapes=[pltpu.VMEM((tq // NSA_SQ * (seq // SLC_LEN), NSA_SQ), F32)],
        compiler_params=_params(("parallel", "parallel", "arbitrary"), VMEM_MIXER_MIB),
        name="nsa_attn",
    )(p_nsa, cosf, sins, kcmp, vcmp, ksr, vsb, kwr, vwb, p_sm)


def _rope_tables(seq):
    inv = ROPE_THETA ** (-jnp.arange(0, HEAD_DIM, 2, dtype=F32) / HEAD_DIM)
    ang = jnp.arange(seq, dtype=F32)[:, None] * inv[None, :]
    cos, sin = jnp.cos(ang), jnp.sin(ang)
    return jnp.concatenate([cos, cos], axis=-1), jnp.concatenate([-sin, sin], axis=-1)


def _mixers(p_hg, p_gd, p_nsa, p_sm, lb, hg_norm, conv_w, a_log, dt_bias, gd_norm,
            posk, w1k, w2k, posv, w1v, w2v, cosf, sins, w_down, layer):
    o_a, o_b, w_dn = _hgrn_gdn(p_hg, lb, hg_norm, p_gd, p_sm, conv_w, a_log, dt_bias, gd_norm, w_down, layer)
    prep = _nsa_prep(p_nsa, cosf, sins, posk, w1k.astype(BF16), w2k.astype(BF16),
                     posv, w1v.astype(BF16), w2v.astype(BF16))
    o_c = _nsa_attn(p_nsa, p_sm, cosf, sins, *prep)
    return (o_a, o_b, o_c), w_dn


def kernel(x, attn_norm, w_in, w_out, ffn_norm, w_gate, w_up, w_down, final_norm, hgrn_lb_logits, hgrn_out_norm, gdn_conv, gdn_A_log, gdn_dt_bias, gdn_out_norm, cmp_pos_k, cmp_w1_k, cmp_w2_k, cmp_pos_v, cmp_w1_v, cmp_w2_v):
    bsz, seq, d = x.shape
    depth = w_in.shape[0]
    n = bsz * seq
    cosf, sins = _rope_tables(seq)
    lb_p = jax.nn.softmax(hgrn_lb_logits.astype(F32), axis=0)
    lb_all = jnp.concatenate([jnp.zeros_like(lb_p[:1]), jnp.cumsum(lb_p, axis=0)[:-1]], axis=0)

    c_gd = 4 * HG_W
    c_ab = c_gd + 4 * GD_W
    c_nsa = c_ab + 2 * GD_HEADS
    c_gate = c_nsa + NSA_W + 6 * NSA_KV_W
    n_gate = NSA_HEADS * 3

    w_t = jnp.swapaxes(w_in, 1, 2)
    w_nsa = w_t[:, c_nsa:c_gate, :]
    w_sm = jnp.concatenate(
        [w_t[:, c_ab:c_nsa, :], w_t[:, c_gate:, :],
         jnp.zeros((depth, SM_W - 2 * GD_HEADS - n_gate, d), F32)], axis=1)

    xf = x.reshape(n, d)
    xw, ss = _norm_inputs(xf, attn_norm[0])
    for l in range(depth):
        p_hg = _matmul_nt(xw, ss, w_t, l, 0, c_gd).reshape(bsz, seq, -1)
        p_gd = _matmul_nt(xw, ss, w_t, l, c_gd, c_ab - c_gd).reshape(bsz, seq, -1)
        p_nsa = _matmul_nt(xw, ss, w_nsa, l).reshape(bsz, seq, -1)
        p_sm = _matmul_nt(xw, ss, w_sm, l).reshape(bsz, seq, -1)
        mix, w_dn = _mixers(p_hg, p_gd, p_nsa, p_sm, lb_all[l], hgrn_out_norm[l], gdn_conv[l],
                            gdn_A_log[l], gdn_dt_bias[l], gdn_out_norm[l],
                            cmp_pos_k[l], cmp_w1_k[l], cmp_w2_k[l], cmp_pos_v[l], cmp_w1_v[l], cmp_w2_v[l],
                            cosf, sins, w_down, l)
        xf, xw, ss = _out_proj([o.reshape(n, o.shape[-1]) for o in mix], w_out, l, xf, ffn_norm[l])
        act = _gate_up(xw, ss, w_gate, w_up, l)
        next_norm = attn_norm[l + 1] if l + 1 < depth else final_norm
        xf, xw, ss = _matmul_ktiled_res(act, w_dn[None], 0, xf, next_norm)
    return _rmsnorm(xf, final_norm, F32).reshape(bsz, seq, d)
```
